```python
import math
import jax, jax.numpy as jnp
from jax import lax
import numpy as np

D_MODEL = 1024
BATCH = 4
SEQ = 8192
DEPTH = 1
DEC_BATCH = 8
DEC_SEQ = 8192
PAST_LEN = 128

GRID_W = 64
Q_BLOCK = 128
EPS = 1e-6
A_HEADS = 8
A_HEAD_DIM = 64
A_V_DIM = 2 * A_HEAD_DIM
A_ROT_DIM = A_HEAD_DIM // 4
A_ROPE_THETA = 500000.0
B_HEADS = 16
B_KV_HEADS = 4
B_GROUP = B_HEADS // B_KV_HEADS
B_HEAD_DIM = 64
B_ROPE_THETA = 10000.0
D_FF = 2816
CONV_W = 3

A_Q = A_HEADS * 2 * A_HEAD_DIM
A_K = A_HEADS * 2 * A_HEAD_DIM
A_V = A_HEADS * A_V_DIM
B_Q = B_HEADS * B_HEAD_DIM
B_K = B_KV_HEADS * B_HEAD_DIM
B_V = B_KV_HEADS * B_HEAD_DIM
GATE_W = 2 * D_MODEL
D_IN = A_Q + A_K + A_V + B_Q + B_K + B_V + GATE_W
SPLIT_IDX = (A_Q, A_Q + A_K, A_Q + A_K + A_V, A_Q + A_K + A_V + B_Q, A_Q + A_K + A_V + B_Q + B_K, A_Q + A_K + A_V + B_Q + B_K + B_V)

kernel_name = 'hybrid_diffattn_gqa2d_convffn_encoder'


def _rmsnorm(x, g):
    x32 = x.astype(jnp.float32)
    y = x32 * lax.rsqrt(jnp.mean(x32 * x32, axis=-1, keepdims=True) + EPS)
    return (y * g.astype(jnp.float32)).astype(x.dtype)


def _rope_angles(pos, dim, theta):
    inv = 1.0 / (theta ** (jnp.arange(0, dim, 2, dtype=jnp.float32) / dim))
    ang = pos.astype(jnp.float32)[:, None] * inv[None, :]
    return jnp.cos(ang), jnp.sin(ang)


def _rotate(x, cos, sin):
    c = cos[None, :, None, :].astype(x.dtype)
    s = sin[None, :, None, :].astype(x.dtype)
    x1, x2 = jnp.split(x, 2, axis=-1)
    return jnp.concatenate([x1 * c - x2 * s, x2 * c + x1 * s], axis=-1)


def _diff_attention(q, k, v, lam):
    B, S = q.shape[0], q.shape[1]
    nb = S // Q_BLOCK
    qb = q.reshape(B, nb, Q_BLOCK, A_HEADS, 2, A_HEAD_DIM).swapaxes(0, 1)
    scale = A_HEAD_DIM ** -0.5

    def block(qi):
        s = jnp.einsum('bqhcd,bkhcd->bhcqk', qi, k).astype(jnp.float32) * scale
        p = jax.nn.softmax(s, axis=-1)
        w = (p[:, :, 0] - lam * p[:, :, 1]).astype(v.dtype)
        return jnp.einsum('bhqk,bkhd->bqhd', w, v)

    o = lax.map(block, qb)
    return o.swapaxes(0, 1).reshape(B, S, A_HEADS, A_V_DIM)


def _gqa_attention(q, k, v):
    B, S = q.shape[0], q.shape[1]
    nb = S // Q_BLOCK
    qb = q.reshape(B, nb, Q_BLOCK, B_KV_HEADS, B_GROUP, B_HEAD_DIM).swapaxes(0, 1)
    scale = B_HEAD_DIM ** -0.5

    def block(qi):
        s = jnp.einsum('bqngd,bknd->bngqk', qi, k).astype(jnp.float32) * scale
        p = jax.nn.softmax(s, axis=-1).astype(v.dtype)
        return jnp.einsum('bngqk,bknd->bqngd', p, v)

    o = lax.map(block, qb)
    return o.swapaxes(0, 1).reshape(B, S, B_HEADS * B_HEAD_DIM)


def _layer(x, layer_idx, norm_mix_g, w_in, lam_q1, lam_k1, lam_q2, lam_k2, subln_g,
           q_norm_g, k_norm_g, w_proj_a, w_proj_b, w_out, norm_ffn_g, w_up, conv_w, conv_b, w_down):
    B, S, _ = x.shape
    rows_n = S // GRID_W
    h = _rmsnorm(x, norm_mix_g)
    proj = h @ w_in
    qa, ka, va, qg, kg, vg, gates = jnp.split(proj, SPLIT_IDX, axis=-1)

    pos = jnp.arange(S)
    cos_a, sin_a = _rope_angles(pos, A_ROT_DIM, A_ROPE_THETA)

    def partial_rope(t):
        return jnp.concatenate([_rotate(t[..., :A_ROT_DIM], cos_a, sin_a), t[..., A_ROT_DIM:]], axis=-1)

    qa = partial_rope(qa.reshape(B, S, 2 * A_HEADS, A_HEAD_DIM)).reshape(B, S, A_HEADS, 2, A_HEAD_DIM)
    ka = partial_rope(ka.reshape(B, S, 2 * A_HEADS, A_HEAD_DIM)).reshape(B, S, A_HEADS, 2, A_HEAD_DIM)
    va = va.reshape(B, S, A_HEADS, A_V_DIM)
    lam_init = 0.8 - 0.6 * math.exp(-0.3 * layer_idx)
    lam = (jnp.exp(jnp.sum(lam_q1.astype(jnp.float32) * lam_k1.astype(jnp.float32)))
           - jnp.exp(jnp.sum(lam_q2.astype(jnp.float32) * lam_k2.astype(jnp.float32))) + lam_init)
    oa = _diff_attention(qa, ka, va, lam)
    oa = (_rmsnorm(oa, subln_g) * (1.0 - lam_init)).reshape(B, S, A_HEADS * A_V_DIM)

    rows = jnp.repeat(jnp.arange(rows_n), GRID_W)
    cols = jnp.tile(jnp.arange(GRID_W), rows_n)
    half = B_HEAD_DIM // 2
    cos_r, sin_r = _rope_angles(rows, half, B_ROPE_THETA)
    cos_c, sin_c = _rope_angles(cols, half, B_ROPE_THETA)

    def axial_rope(t):
        return jnp.concatenate([_rotate(t[..., :half], cos_r, sin_r), _rotate(t[..., half:], cos_c, sin_c)], axis=-1)

    qg = axial_rope(_rmsnorm(qg.reshape(B, S, B_HEADS, B_HEAD_DIM), q_norm_g))
    qg = qg.reshape(B, S, B_KV_HEADS, B_GROUP, B_HEAD_DIM)
    kg = axial_rope(_rmsnorm(kg.reshape(B, S, B_KV_HEADS, B_HEAD_DIM), k_norm_g))
    vg = vg.reshape(B, S, B_KV_HEADS, B_HEAD_DIM)
    ob = _gqa_attention(qg, kg, vg)

    g_a, g_b = jnp.split(gates, 2, axis=-1)
    mix = jax.nn.sigmoid(g_a) * (oa @ w_proj_a) + jax.nn.sigmoid(g_b) * (ob @ w_proj_b)
    x = x + mix @ w_out

    h = _rmsnorm(x, norm_ffn_g)
    u = h @ w_up
    up = jnp.pad(u, ((0, 0), (1, 1), (0, 0)))
    u = up[:, :-2] * conv_w[0] + up[:, 1:-1] * conv_w[1] + up[:, 2:] * conv_w[2] + conv_b
    val, gate = jnp.split(u, 2, axis=-1)
    return x + (jax.nn.silu(gate) * val) @ w_down


def _trunk(x, norm_mix_g, w_in, lam_q1, lam_k1, lam_q2, lam_k2, subln_g, q_norm_g, k_norm_g,
           w_proj_a, w_proj_b, w_out, norm_ffn_g, w_up, conv_w, conv_b, w_down, norm_final_g):
    for l in range(DEPTH):
        x = _layer(x, l, norm_mix_g[l], w_in[l], lam_q1[l], lam_k1[l], lam_q2[l], lam_k2[l], subln_g[l],
                   q_norm_g[l], k_norm_g[l], w_proj_a[l], w_proj_b[l], w_out[l], norm_ffn_g[l],
                   w_up[l], conv_w[l], conv_b[l], w_down[l])
    return _rmsnorm(x, norm_final_g)


def setup_inputs(seed: int = 0) -> dict:
    key = jax.random.key(seed)
    ks = jax.random.split(key, 20)
    f32 = jnp.float32

    def nrm(k, shape, scale):
        return jax.random.normal(k, shape, f32) * scale

    def gain(k, shape):
        return 1.0 + 0.02 * jax.random.normal(k, shape, f32)

    return {
        'x_prompt': nrm(ks[0], (BATCH, SEQ, D_MODEL), 1.0),
        'x_sample': nrm(ks[1], (DEC_BATCH, DEC_SEQ, D_MODEL), 1.0),
        'norm_mix_g': gain(ks[2], (DEPTH, D_MODEL)),
        'w_in': nrm(ks[3], (DEPTH, D_MODEL, D_IN), D_MODEL ** -0.5),
        'lam_q1': nrm(ks[4], (DEPTH, A_HEAD_DIM), 0.1),
        'lam_k1': nrm(ks[5], (DEPTH, A_HEAD_DIM), 0.1),
        'lam_q2': nrm(ks[6], (DEPTH, A_HEAD_DIM), 0.1),
        'lam_k2': nrm(ks[7], (DEPTH, A_HEAD_DIM), 0.1),
        'subln_g': gain(ks[8], (DEPTH, A_V_DIM)),
        'q_norm_g': gain(ks[9], (DEPTH, B_HEAD_DIM)),
        'k_norm_g': gain(ks[10], (DEPTH, B_HEAD_DIM)),
        'w_proj_a': nrm(ks[11], (DEPTH, A_HEADS * A_V_DIM, D_MODEL), (A_HEADS * A_V_DIM) ** -0.5),
        'w_proj_b': nrm(ks[12], (DEPTH, B_HEADS * B_HEAD_DIM, D_MODEL), (B_HEADS * B_HEAD_DIM) ** -0.5),
        'w_out': nrm(ks[13], (DEPTH, D_MODEL, D_MODEL), D_MODEL ** -0.5),
        'norm_ffn_g': gain(ks[14], (DEPTH, D_MODEL)),
        'w_up': nrm(ks[15], (DEPTH, D_MODEL, 2 * D_FF), D_MODEL ** -0.5),
        'conv_w': nrm(ks[16], (DEPTH, CONV_W, 2 * D_FF), CONV_W ** -0.5),
        'conv_b': nrm(ks[17], (DEPTH, 2 * D_FF), 0.01),
        'w_down': nrm(ks[18], (DEPTH, D_FF, D_MODEL), D_FF ** -0.5),
        'norm_final_g': gain(ks[19], (D_MODEL,)),
    }


def reference(x_prompt, x_sample, norm_mix_g, w_in, lam_q1, lam_k1, lam_q2, lam_k2, subln_g,
              q_norm_g, k_norm_g, w_proj_a, w_proj_b, w_out, norm_ffn_g, w_up, conv_w, conv_b,
              w_down, norm_final_g):
    y_prompt = _trunk(x_prompt, norm_mix_g, w_in, lam_q1, lam_k1, lam_q2, lam_k2, subln_g, q_norm_g,
                      k_norm_g, w_proj_a, w_proj_b, w_out, norm_ffn_g, w_up, conv_w, conv_b, w_down,
                      norm_final_g)
    y_sample = _trunk(x_sample, norm_mix_g, w_in, lam_q1, lam_k1, lam_q2, lam_k2, subln_g, q_norm_g,
                      k_norm_g, w_proj_a, w_proj_b, w_out, norm_ffn_g, w_up, conv_w, conv_b, w_down,
                      norm_final_g)
    return (y_prompt, y_sample)
```

```python
import functools
import math

import jax
import jax.numpy as jnp
from jax import lax
from jax.experimental import pallas as pl
from jax.experimental.pallas import tpu as pltpu

D_MODEL = 1024
GRID_W = 64
EPS = 1e-6
A_HEADS = 8
A_HEAD_DIM = 64
A_V_DIM = 128
A_ROT_DIM = 16
A_ROPE_THETA = 500000.0
B_HEADS = 16
B_KV_HEADS = 4
B_GROUP = 4
B_HEAD_DIM = 64
B_ROPE_THETA = 10000.0
D_FF = 2816
LAM_INIT = 0.8 - 0.6 * math.exp(-0.3 * 0)

A_Q = A_HEADS * 2 * A_HEAD_DIM
A_K = A_Q
A_V = A_HEADS * A_V_DIM
B_Q = B_HEADS * B_HEAD_DIM
B_K = B_KV_HEADS * B_HEAD_DIM
B_V = B_K
GATE_W = 2 * D_MODEL
D_IN = A_Q + A_K + A_V + B_Q + B_K + B_V + GATE_W
C_QA = 0
C_KA = C_QA + A_Q
C_VA = C_KA + A_K
C_QB = C_VA + A_V
C_KB = C_QB + B_Q
C_VB = C_KB + B_K
C_GATE = C_VB + B_V

LANES = 128
HALF = 64
B_PAIRS = B_KV_HEADS // 2
HALO = 16
VMEM_LIMIT = 56 * 1024 * 1024

F32 = jnp.float32
BF16 = jnp.bfloat16


def _rms_rows(x, g):
    return x * lax.rsqrt(jnp.mean(x * x, axis=-1, keepdims=True) + EPS) * g


def _const_spec(shape):
    zeros = (0,) * len(shape)
    return pl.BlockSpec(shape, lambda *_: zeros, pipeline_mode=pl.Buffered(1))


def _inproj_kernel(x_ref, g_ref, w_ref, ca_ref, sa_ref, cb_ref, sb_ref, qg_ref, kg_ref,
                   qat_ref, ka_ref, vat_ref, qbt_ref, kb_ref, vbt_ref, sig_ref):
    tm = x_ref.shape[1]
    tk = vat_ref.shape[4]
    h = _rms_rows(x_ref[0], g_ref[...]).astype(BF16)
    lane = lax.broadcasted_iota(jnp.int32, (tm, LANES), 1)
    lo_half = lane < HALF
    first8 = (lane & 8) == 0
    first16 = (lane & 16) == 0
    ca, sa, cb, sb = ca_ref[...], sa_ref[...], cb_ref[...], sb_ref[...]
    qg, kg = qg_ref[...], kg_ref[...]
    scale = A_HEAD_DIM ** -0.5

    def proj(c0, n):
        return jnp.dot(h, w_ref[:, c0:c0 + n], preferred_element_type=F32)

    def rope_a(t):
        partner = jnp.where(first8, pltpu.roll(t, LANES - 8, 1), pltpu.roll(t, 8, 1))
        return t * ca + partner * sa

    def rope_b(t):
        partner = jnp.where(first16, pltpu.roll(t, LANES - 16, 1), pltpu.roll(t, 16, 1))
        return t * cb + partner * sb

    def headnorm(t, g):
        sq = t * t
        s_lo = jnp.sum(jnp.where(lo_half, sq, 0.0), axis=1, keepdims=True)
        s_hi = jnp.sum(jnp.where(lo_half, 0.0, sq), axis=1, keepdims=True)
        ms = jnp.where(lo_half, s_lo, s_hi) * (1.0 / HALF)
        return t * lax.rsqrt(ms + EPS) * g

    def store_t(ref, idx, t):
        for s in range(tm // tk):
            ref[idx + (s,)] = t[s * tk:(s + 1) * tk].T.astype(BF16)

    for p in range(A_HEADS // 2):
        acc = proj(C_QA + 2 * LANES * p, 2 * LANES)
        for i in range(2):
            t = rope_a(acc[:, i * LANES:(i + 1) * LANES]) * scale
            qat_ref[0, 2 * p + i] = t.T.astype(BF16)
    for p in range(A_HEADS // 2):
        acc = proj(C_KA + 2 * LANES * p, 2 * LANES)
        for i in range(2):
            ka_ref[0, 2 * p + i] = rope_a(acc[:, i * LANES:(i + 1) * LANES]).astype(BF16)
    for p in range(A_HEADS // 2):
        acc = proj(C_VA + 2 * LANES * p, 2 * LANES)
        for i in range(2):
            store_t(vat_ref, (0, 2 * p + i), acc[:, i * LANES:(i + 1) * LANES])
    for p in range(B_PAIRS * B_GROUP // 2):
        acc = proj(C_QB + 2 * LANES * p, 2 * LANES)
        for i in range(2):
            blk = 2 * p + i
            t = rope_b(headnorm(acc[:, i * LANES:(i + 1) * LANES], qg)) * scale
            qbt_ref[0, blk // B_GROUP, blk % B_GROUP] = t.T.astype(BF16)
    acc = proj(C_KB, B_PAIRS * LANES)
    for j in range(B_PAIRS):
        kb_ref[0, j] = rope_b(headnorm(acc[:, j * LANES:(j + 1) * LANES], kg)).astype(BF16)
    acc = proj(C_VB, B_PAIRS * LANES)
    for j in range(B_PAIRS):
        store_t(vbt_ref, (0, j), acc[:, j * LANES:(j + 1) * LANES])
    for p in range(GATE_W // (2 * LANES)):
        acc = proj(C_GATE + 2 * LANES * p, 2 * LANES)
        sig_ref[0, :, 2 * LANES * p:2 * LANES * (p + 1)] = jax.nn.sigmoid(acc)


def _inproj(x, g, w, tabs, qg, kg, *, tm, tk):
    b, s, _ = x.shape
    nt = s // tk
    grid = (b, s // tm)
    row = lambda bi, i: (i, 0)
    out_shapes = (
        jax.ShapeDtypeStruct((b, A_HEADS, LANES, s), BF16),
        jax.ShapeDtypeStruct((b, A_HEADS, s, LANES), BF16),
        jax.ShapeDtypeStruct((b, A_HEADS, nt, LANES, tk), BF16),
        jax.ShapeDtypeStruct((b, B_PAIRS, B_GROUP, LANES, s), BF16),
        jax.ShapeDtypeStruct((b, B_PAIRS, s, LANES), BF16),
        jax.ShapeDtypeStruct((b, B_PAIRS, nt, LANES, tk), BF16),
        jax.ShapeDtypeStruct((b, s, GATE_W), F32),
    )
    out_specs = (
        pl.BlockSpec((1, A_HEADS, LANES, tm), lambda bi, i: (bi, 0, 0, i)),
        pl.BlockSpec((1, A_HEADS, tm, LANES), lambda bi, i: (bi, 0, i, 0)),
        pl.BlockSpec((1, A_HEADS, tm // tk, LANES, tk), lambda bi, i: (bi, 0, i, 0, 0)),
        pl.BlockSpec((1, B_PAIRS, B_GROUP, LANES, tm), lambda bi, i: (bi, 0, 0, 0, i)),
        pl.BlockSpec((1, B_PAIRS, tm, LANES), lambda bi, i: (bi, 0, i, 0)),
        pl.BlockSpec((1, B_PAIRS, tm // tk, LANES, tk), lambda bi, i: (bi, 0, i, 0, 0)),
        pl.BlockSpec((1, tm, GATE_W), lambda bi, i: (bi, i, 0)),
    )
    in_specs = [
        pl.BlockSpec((1, tm, D_MODEL), lambda bi, i: (bi, i, 0)),
        _const_spec((1, D_MODEL)),
        _const_spec((D_MODEL, D_IN)),
        pl.BlockSpec((tm, LANES), row), pl.BlockSpec((tm, LANES), row),
        pl.BlockSpec((tm, LANES), row), pl.BlockSpec((tm, LANES), row),
        _const_spec((1, LANES)), _const_spec((1, LANES)),
    ]
    return pl.pallas_call(
        _inproj_kernel, grid=grid, in_specs=in_specs, out_specs=out_specs, out_shape=out_shapes,
        compiler_params=pltpu.CompilerParams(
            dimension_semantics=("parallel", "parallel"), vmem_limit_bytes=VMEM_LIMIT),
        name="inproj",
    )(x, g, w, *tabs, qg, kg)


def _flash_step(k, q, vt, m, l, acc):
    s = jnp.dot(k, q, preferred_element_type=F32)
    m_new = jnp.maximum(m, jnp.max(s, axis=0, keepdims=True))
    alpha = jnp.exp(m - m_new)
    p = jnp.exp(s - m_new)
    l_new = alpha * l + jnp.sum(p, axis=0, keepdims=True)
    acc_new = alpha * acc + jnp.dot(vt, p.astype(BF16), preferred_element_type=F32)
    return m_new, l_new, acc_new


def _attn_a_kernel(lam_ref, sg_ref, qt_ref, k_ref, vt_ref, o_ref, acc_ref):
    tq = qt_ref.shape[3]
    nkv, tk = vt_ref.shape[2], vt_ref.shape[4]
    qt = qt_ref[0, 0]
    row = lax.broadcasted_iota(jnp.int32, (LANES, tq), 0)
    zero = jnp.zeros_like(qt)
    qs = (jnp.where(row < HALF, qt, zero), jnp.where(row >= HALF, qt, zero))
    acc_ref[...] = jnp.zeros_like(acc_ref)
    m0 = jnp.full((1, tq), -jnp.inf, F32)
    l0 = jnp.zeros((1, tq), F32)

    def body(j, carry):
        ms, ls = carry
        k = k_ref[0, 0, pl.ds(pl.multiple_of(j * tk, tk), tk), :]
        vt = vt_ref[0, 0, j]
        out_m, out_l = [], []
        for c in range(2):
            m, l, acc = _flash_step(k, qs[c], vt, ms[c], ls[c], acc_ref[c])
            acc_ref[c] = acc
            out_m.append(m)
            out_l.append(l)
        return tuple(out_m), tuple(out_l)

    _, ls = lax.fori_loop(0, nkv, body, ((m0, m0), (l0, l0)))
    lam_v = lam_ref[...]
    lam = (jnp.exp(jnp.sum(lam_v[0:1] * lam_v[1:2], axis=1, keepdims=True))
           - jnp.exp(jnp.sum(lam_v[2:3] * lam_v[3:4], axis=1, keepdims=True)) + LAM_INIT)
    o = acc_ref[0] / ls[0] - lam * (acc_ref[1] / ls[1])
    y = _rms_rows(o.T, sg_ref[...]) * (1.0 - LAM_INIT)
    o_ref[0] = y.astype(BF16)


def _attn_a(lam4, sg, qat, ka, vat, *, tq):
    b, _, _, s = qat.shape
    nkv, tk = vat.shape[2], vat.shape[4]
    return pl.pallas_call(
        _attn_a_kernel,
        grid=(b, A_HEADS, s // tq),
        in_specs=[
            _const_spec((4, A_HEAD_DIM)), _const_spec((1, A_V_DIM)),
            pl.BlockSpec((1, 1, LANES, tq), lambda bi, h, i: (bi, h, 0, i)),
            pl.BlockSpec((1, 1, s, LANES), lambda bi, h, i: (bi, h, 0, 0)),
            pl.BlockSpec((1, 1, nkv, LANES, tk), lambda bi, h, i: (bi, h, 0, 0, 0)),
        ],
        out_specs=pl.BlockSpec((1, tq, A_V_DIM), lambda bi, h, i: (bi, i, h)),
        out_shape=jax.ShapeDtypeStruct((b, s, A_HEADS * A_V_DIM), BF16),
        scratch_shapes=[pltpu.VMEM((2, A_V_DIM, tq), F32)],
        compiler_params=pltpu.CompilerParams(
            dimension_semantics=("parallel", "parallel", "arbitrary"), vmem_limit_bytes=VMEM_LIMIT),
        name="attn_a",
    )(lam4, sg, qat, ka, vat)


def _attn_b_kernel(qt_ref, k_ref, vt_ref, o_ref, acc_ref):
    tq = qt_ref.shape[4]
    nkv, tk = vt_ref.shape[2], vt_ref.shape[4]
    nh = 2 * B_GROUP
    row = lax.broadcasted_iota(jnp.int32, (LANES, tq), 0)
    qs = []
    for half in range(2):
        for g in range(B_GROUP):
            qt = qt_ref[0, 0, g]
            keep = (row < HALF) if half == 0 else (row >= HALF)
            qs.append(jnp.where(keep, qt, jnp.zeros_like(qt)))
    acc_ref[...] = jnp.zeros_like(acc_ref)
    m0 = jnp.full((1, tq), -jnp.inf, F32)
    l0 = jnp.zeros((1, tq), F32)

    def body(j, carry):
        ms, ls = carry
        k = k_ref[0, 0, pl.ds(pl.multiple_of(j * tk, tk), tk), :]
        vt = vt_ref[0, 0, j]
        out_m, out_l = [], []
        for idx in range(nh):
            half = idx // B_GROUP
            m, l, acc = _flash_step(k, qs[idx], vt[half * HALF:(half + 1) * HALF], ms[idx], ls[idx],
                                    acc_ref[idx])
            acc_ref[idx] = acc
            out_m.append(m)
            out_l.append(l)
        return tuple(out_m), tuple(out_l)

    _, ls = lax.fori_loop(0, nkv, body, ((m0,) * nh, (l0,) * nh))
    for blk in range(nh // 2):
        ot = jnp.concatenate([acc_ref[2 * blk] / ls[2 * blk], acc_ref[2 * blk + 1] / ls[2 * blk + 1]], axis=0)
        o_ref[0, :, blk * LANES:(blk + 1) * LANES] = ot.T.astype(BF16)


def _attn_b(qbt, kb, vbt, *, tq):
    b, _, _, _, s = qbt.shape
    nkv, tk = vbt.shape[2], vbt.shape[4]
    width = 2 * B_GROUP * B_HEAD_DIM
    return pl.pallas_call(
        _attn_b_kernel,
        grid=(b, B_PAIRS, s // tq),
        in_specs=[
            pl.BlockSpec((1, 1, B_GROUP, LANES, tq), lambda bi, j, i: (bi, j, 0, 0, i)),
            pl.BlockSpec((1, 1, s, LANES), lambda bi, j, i: (bi, j, 0, 0)),
            pl.BlockSpec((1, 1, nkv, LANES, tk), lambda bi, j, i: (bi, j, 0, 0, 0)),
        ],
        out_specs=pl.BlockSpec((1, tq, width), lambda bi, j, i: (bi, i, j)),
        out_shape=jax.ShapeDtypeStruct((b, s, B_HEADS * B_HEAD_DIM), BF16),
        scratch_shapes=[pltpu.VMEM((2 * B_GROUP, B_HEAD_DIM, tq), F32)],
        compiler_params=pltpu.CompilerParams(
            dimension_semantics=("parallel", "parallel", "arbitrary"), vmem_limit_bytes=VMEM_LIMIT),
        name="attn_b",
    )(qbt, kb, vbt)


def _merge_kernel(x_ref, oa_ref, ob_ref, sig_ref, wa_ref, wb_ref, wo_ref, o_ref):
    pa = jnp.dot(oa_ref[0], wa_ref[...], preferred_element_type=F32)
    pb = jnp.dot(ob_ref[0], wb_ref[...], preferred_element_type=F32)
    mix = sig_ref[0, :, :D_MODEL] * pa + sig_ref[0, :, D_MODEL:] * pb
    o_ref[0] = x_ref[0] + jnp.dot(mix.astype(BF16), wo_ref[...], preferred_element_type=F32)


def _merge(x, oa, ob, sig, wa, wb, wo, *, tm):
    b, s, _ = x.shape
    rows = lambda width: pl.BlockSpec((1, tm, width), lambda bi, i: (bi, i, 0))
    wspec = _const_spec((D_MODEL, D_MODEL))
    return pl.pallas_call(
        _merge_kernel,
        grid=(b, s // tm),
        in_specs=[rows(D_MODEL), rows(D_MODEL), rows(D_MODEL), rows(GATE_W), wspec, wspec, wspec],
        out_specs=rows(D_MODEL),
        out_shape=jax.ShapeDtypeStruct((b, s, D_MODEL), F32),
        compiler_params=pltpu.CompilerParams(
            dimension_semantics=("parallel", "parallel"), vmem_limit_bytes=VMEM_LIMIT),
        name="merge",
    )(x, oa, ob, sig, wa, wb, wo)


def _ffn_kernel(xp_ref, x_ref, xn_ref, g_ref, wu_ref, cw_ref, cb_ref, wd_ref, gf_ref, o_ref, h_ref, *, ck):
    tm = x_ref.shape[1]
    i = pl.program_id(1)
    g = g_ref[...]
    x = x_ref[0]
    hp = jnp.where(i > 0, _rms_rows(xp_ref[0], g), 0.0)
    hn = jnp.where(i < pl.num_programs(1) - 1, _rms_rows(xn_ref[0], g), 0.0)
    h_ref[0:HALO] = hp.astype(BF16)
    h_ref[HALO:HALO + tm] = _rms_rows(x, g).astype(BF16)
    h_ref[HALO + tm:] = hn.astype(BF16)
    h = h_ref[...]

    def conv(u, c0):
        w = cw_ref[:, c0:c0 + ck]
        return (u[HALO - 1:HALO - 1 + tm] * w[0:1] + u[HALO:HALO + tm] * w[1:2]
                + u[HALO + 1:HALO + 1 + tm] * w[2:3] + cb_ref[:, c0:c0 + ck])

    acc = jnp.zeros((tm, D_MODEL), F32)
    for c in range(D_FF // ck):
        val = conv(jnp.dot(h, wu_ref[:, c * ck:(c + 1) * ck], preferred_element_type=F32), c * ck)
        gate = conv(jnp.dot(h, wu_ref[:, D_FF + c * ck:D_FF + (c + 1) * ck], preferred_element_type=F32),
                    D_FF + c * ck)
        act = (jax.nn.silu(gate) * val).astype(BF16)
        acc = acc + jnp.dot(act, wd_ref[c * ck:(c + 1) * ck, :], preferred_element_type=F32)
    o_ref[0] = _rms_rows(x + acc, gf_ref[...])


def _ffn(x, g, wu, cw, cb, wd, gf, *, tm, ck):
    b, s, _ = x.shape
    nh = tm // HALO
    last = s // HALO - 1
    return pl.pallas_call(
        functools.partial(_ffn_kernel, ck=ck),
        grid=(b, s // tm),
        in_specs=[
            pl.BlockSpec((1, HALO, D_MODEL), lambda bi, i: (bi, jnp.maximum(i * nh - 1, 0), 0)),
            pl.BlockSpec((1, tm, D_MODEL), lambda bi, i: (bi, i, 0)),
            pl.BlockSpec((1, HALO, D_MODEL), lambda bi, i: (bi, jnp.minimum((i + 1) * nh, last), 0)),
            _const_spec((1, D_MODEL)),
            _const_spec((D_MODEL, 2 * D_FF)),
            _const_spec((3, 2 * D_FF)),
            _const_spec((1, 2 * D_FF)),
            _const_spec((D_FF, D_MODEL)),
            _const_spec((1, D_MODEL)),
        ],
        out_specs=pl.BlockSpec((1, tm, D_MODEL), lambda bi, i: (bi, i, 0)),
        out_shape=jax.ShapeDtypeStruct((b, s, D_MODEL), F32),
        scratch_shapes=[pltpu.VMEM((tm + 2 * HALO, D_MODEL), BF16)],
        compiler_params=pltpu.CompilerParams(
            dimension_semantics=("parallel", "parallel"), vmem_limit_bytes=VMEM_LIMIT),
        name="ffn",
    )(x, x, x, g, wu, cw, cb, wd, gf)


def _rope_angles(pos, dim, theta):
    inv = 1.0 / (theta ** (jnp.arange(0, dim, 2, dtype=F32) / dim))
    ang = pos.astype(F32)[:, None] * inv[None, :]
    return jnp.cos(ang), jnp.sin(ang)


def _rope_tables(s):
    pos = jnp.arange(s)
    ca, sa = _rope_angles(pos, A_ROT_DIM, A_ROPE_THETA)
    ones = jnp.ones((s, A_HEAD_DIM - A_ROT_DIM), F32)
    cos_a = jnp.concatenate([ca, ca, ones], axis=1)
    sin_a = jnp.concatenate([-sa, sa, 0.0 * ones], axis=1)
    half = B_HEAD_DIM // 2
    cr, sr = _rope_angles(pos // GRID_W, half, B_ROPE_THETA)
    cc, sc = _rope_angles(pos % GRID_W, half, B_ROPE_THETA)
    cos_b = jnp.concatenate([cr, cr, cc, cc], axis=1)
    sin_b = jnp.concatenate([-sr, sr, -sc, sc], axis=1)
    two = lambda t: jnp.concatenate([t, t], axis=1)
    return two(cos_a), two(sin_a), two(cos_b), two(sin_b)


def _w_in_columns():
    cols = list(range(C_QB))
    for j in range(B_PAIRS):
        for g in range(B_GROUP):
            for half in range(2):
                head = (2 * j + half) * B_GROUP + g
                cols.extend(range(C_QB + head * B_HEAD_DIM, C_QB + (head + 1) * B_HEAD_DIM))
    cols.extend(range(C_KB, D_IN))
    return jnp.asarray(cols, dtype=jnp.int32)


def _tile(n, pref):
    return pref if n % pref == 0 else n


def _trunk(x, p, tabs):
    s = x.shape[1]
    tm = _tile(s, 256)
    tk = _tile(s, 512) if s >= 512 else tm
    tm1 = max(tm, tk)
    qat, ka, vat, qbt, kb, vbt, sig = _inproj(x, p["norm_mix_g"], p["w_in"], tabs, p["q_norm_g"], p["k_norm_g"],
                                              tm=tm1, tk=tk)
    oa = _attn_a(p["lam4"], p["subln_g"], qat, ka, vat, tq=tm)
    ob = _attn_b(qbt, kb, vbt, tq=tm)
    x1 = _merge(x, oa, ob, sig, p["w_proj_a"], p["w_proj_b"], p["w_out"], tm=_tile(s, 512))
    return _ffn(x1, p["norm_ffn_g"], p["w_up"], p["conv_w"], p["conv_b"], p["w_down"], p["norm_final_g"],
                tm=_tile(s, 512), ck=256)


def kernel(x_prompt, x_sample, norm_mix_g, w_in, lam_q1, lam_k1, lam_q2, lam_k2, subln_g, q_norm_g, k_norm_g,
           w_proj_a, w_proj_b, w_out, norm_ffn_g, w_up, conv_w, conv_b, w_down, norm_final_g):
    two = lambda t: jnp.concatenate([t, t]).reshape(1, LANES).astype(F32)
    p = {
        "norm_mix_g": norm_mix_g[0].reshape(1, D_MODEL),
        "w_in": jnp.take(w_in[0], _w_in_columns(), axis=1).astype(BF16),
        "lam4": jnp.stack([lam_q1[0], lam_k1[0], lam_q2[0], lam_k2[0]]).astype(F32),
        "subln_g": subln_g[0].reshape(1, A_V_DIM),
        "q_norm_g": two(q_norm_g[0]),
        "k_norm_g": two(k_norm_g[0]),
        "w_proj_a": w_proj_a[0].astype(BF16),
        "w_proj_b": w_proj_b[0].astype(BF16),
        "w_out": w_out[0].astype(BF16),
        "norm_ffn_g": norm_ffn_g[0].reshape(1, D_MODEL),
        "w_up": w_up[0].astype(BF16),
        "conv_w": conv_w[0],
        "conv_b": conv_b[0].reshape(1, 2 * D_FF),
        "w_down": w_down[0].astype(BF16),
        "norm_final_g": norm_final_g.reshape(1, D_MODEL),
    }
    outs = []
    for x in (x_prompt, x_sample):
        outs.append(_trunk(x, p, _rope_tables(x.shape[1])))
    return tuple(outs)
```

```python
import functools
import math

import jax
import jax.numpy as jnp
from jax import lax
from jax.experimental import pallas as pl
from jax.experimental.pallas import tpu as pltpu

D_MODEL = 1024
GRID_W = 64
EPS = 1e-6
A_HEADS = 8
A_HEAD_DIM = 64
A_V_DIM = 128
A_ROT_DIM = 16
A_ROPE_THETA = 500000.0
B_HEADS = 16
B_KV_HEADS = 4
B_GROUP = 4
B_HEAD_DIM = 64
B_ROPE_THETA = 10000.0
D_FF = 2816
LAM_INIT = 0.8 - 0.6 * math.exp(-0.3 * 0)

A_Q = A_HEADS * 2 * A_HEAD_DIM
A_K = A_Q
A_V = A_HEADS * A_V_DIM
B_Q = B_HEADS * B_HEAD_DIM
B_K = B_KV_HEADS * B_HEAD_DIM
B_V = B_K
GATE_W = 2 * D_MODEL
D_IN = A_Q + A_K + A_V + B_Q + B_K + B_V + GATE_W
C_QA = 0
C_KA = C_QA + A_Q
C_VA = C_KA + A_K
C_QB = C_VA + A_V
C_KB = C_QB + B_Q
C_VB = C_KB + B_K
C_GATE = C_VB + B_V

LANES = 128
HALF = 64
B_PAIRS = B_KV_HEADS // 2
HALO = 16
ONES_ROWS = 16
VA_ROWS = A_V_DIM + ONES_ROWS
VB_ROWS = 2 * (B_HEAD_DIM + ONES_ROWS)
VMEM_LIMIT = 56 * 1024 * 1024

F32 = jnp.float32
BF16 = jnp.bfloat16


def _rms_rows(x, g):
    return x * lax.rsqrt(jnp.mean(x * x, axis=-1, keepdims=True) + EPS) * g


def _const_spec(shape):
    zeros = (0,) * len(shape)
    return pl.BlockSpec(shape, lambda *_: zeros, pipeline_mode=pl.Buffered(1))


def _inproj_kernel(x_ref, g_ref, w_ref, ca_ref, sa_ref, cb_ref, sb_ref, qg_ref, kg_ref,
                   qat_ref, ka_ref, vat_ref, qbt_ref, kb_ref, vbt_ref, sig_ref):
    tm = x_ref.shape[1]
    tk = vat_ref.shape[4]
    h = _rms_rows(x_ref[0], g_ref[...]).astype(BF16)
    lane = lax.broadcasted_iota(jnp.int32, (tm, LANES), 1)
    lo_half = lane < HALF
    first8 = (lane & 8) == 0
    first16 = (lane & 16) == 0
    ca, sa, cb, sb = ca_ref[...], sa_ref[...], cb_ref[...], sb_ref[...]
    qg, kg = qg_ref[...], kg_ref[...]
    scale = A_HEAD_DIM ** -0.5

    def proj(c0, n):
        return jnp.dot(h, w_ref[:, c0:c0 + n], preferred_element_type=F32)

    def rope_a(t):
        partner = jnp.where(first8, pltpu.roll(t, LANES - 8, 1), pltpu.roll(t, 8, 1))
        return t * ca + partner * sa

    def rope_b(t):
        partner = jnp.where(first16, pltpu.roll(t, LANES - 16, 1), pltpu.roll(t, 16, 1))
        return t * cb + partner * sb

    def headnorm(t, g):
        sq = t * t
        s_lo = jnp.sum(jnp.where(lo_half, sq, 0.0), axis=1, keepdims=True)
        s_hi = jnp.sum(jnp.where(lo_half, 0.0, sq), axis=1, keepdims=True)
        ms = jnp.where(lo_half, s_lo, s_hi) * (1.0 / HALF)
        return t * lax.rsqrt(ms + EPS) * g

    ones_rows = jnp.ones((ONES_ROWS, tk), BF16)

    def store_vt(ref, idx, t, dv):
        for s in range(tm // tk):
            tt = t[s * tk:(s + 1) * tk].T.astype(BF16)
            for grp in range(LANES // dv):
                r0 = grp * (dv + ONES_ROWS)
                ref[idx + (s, slice(r0, r0 + dv))] = tt[grp * dv:(grp + 1) * dv]
                ref[idx + (s, slice(r0 + dv, r0 + dv + ONES_ROWS))] = ones_rows

    for p in range(A_HEADS // 2):
        acc = proj(C_QA + 2 * LANES * p, 2 * LANES)
        for i in range(2):
            t = rope_a(acc[:, i * LANES:(i + 1) * LANES]) * scale
            qat_ref[0, 2 * p + i] = t.T.astype(BF16)
    for p in range(A_HEADS // 2):
        acc = proj(C_KA + 2 * LANES * p, 2 * LANES)
        for i in range(2):
            ka_ref[0, 2 * p + i] = rope_a(acc[:, i * LANES:(i + 1) * LANES]).astype(BF16)
    for p in range(A_HEADS // 2):
        acc = proj(C_VA + 2 * LANES * p, 2 * LANES)
        for i in range(2):
            store_vt(vat_ref, (0, 2 * p + i), acc[:, i * LANES:(i + 1) * LANES], A_V_DIM)
    for p in range(B_PAIRS * B_GROUP // 2):
        acc = proj(C_QB + 2 * LANES * p, 2 * LANES)
        for i in range(2):
            blk = 2 * p + i
            t = rope_b(headnorm(acc[:, i * LANES:(i + 1) * LANES], qg)) * scale
            qbt_ref[0, blk // B_GROUP, blk % B_GROUP] = t.T.astype(BF16)
    acc = proj(C_KB, B_PAIRS * LANES)
    for j in range(B_PAIRS):
        kb_ref[0, j] = rope_b(headnorm(acc[:, j * LANES:(j + 1) * LANES], kg)).astype(BF16)
    acc = proj(C_VB, B_PAIRS * LANES)
    for j in range(B_PAIRS):
        store_vt(vbt_ref, (0, j), acc[:, j * LANES:(j + 1) * LANES], B_HEAD_DIM)
    for p in range(GATE_W // (2 * LANES)):
        acc = proj(C_GATE + 2 * LANES * p, 2 * LANES)
        sig_ref[0, :, 2 * LANES * p:2 * LANES * (p + 1)] = jax.nn.sigmoid(acc)


def _inproj(x, g, w, tabs, qg, kg, *, tm, tk):
    b, s, _ = x.shape
    nt = s // tk
    grid = (b, s // tm)
    row = lambda bi, i: (i, 0)
    out_shapes = (
        jax.ShapeDtypeStruct((b, A_HEADS, LANES, s), BF16),
        jax.ShapeDtypeStruct((b, A_HEADS, s, LANES), BF16),
        jax.ShapeDtypeStruct((b, A_HEADS, nt, VA_ROWS, tk), BF16),
        jax.ShapeDtypeStruct((b, B_PAIRS, B_GROUP, LANES, s), BF16),
        jax.ShapeDtypeStruct((b, B_PAIRS, s, LANES), BF16),
        jax.ShapeDtypeStruct((b, B_PAIRS, nt, VB_ROWS, tk), BF16),
        jax.ShapeDtypeStruct((b, s, GATE_W), F32),
    )
    out_specs = (
        pl.BlockSpec((1, A_HEADS, LANES, tm), lambda bi, i: (bi, 0, 0, i)),
        pl.BlockSpec((1, A_HEADS, tm, LANES), lambda bi, i: (bi, 0, i, 0)),
        pl.BlockSpec((1, A_HEADS, tm // tk, VA_ROWS, tk), lambda bi, i: (bi, 0, i, 0, 0)),
        pl.BlockSpec((1, B_PAIRS, B_GROUP, LANES, tm), lambda bi, i: (bi, 0, 0, 0, i)),
        pl.BlockSpec((1, B_PAIRS, tm, LANES), lambda bi, i: (bi, 0, i, 0)),
        pl.BlockSpec((1, B_PAIRS, tm // tk, VB_ROWS, tk), lambda bi, i: (bi, 0, i, 0, 0)),
        pl.BlockSpec((1, tm, GATE_W), lambda bi, i: (bi, i, 0)),
    )
    in_specs = [
        pl.BlockSpec((1, tm, D_MODEL), lambda bi, i: (bi, i, 0)),
        _const_spec((1, D_MODEL)),
        _const_spec((D_MODEL, D_IN)),
        pl.BlockSpec((tm, LANES), row), pl.BlockSpec((tm, LANES), row),
        pl.BlockSpec((tm, LANES), row), pl.BlockSpec((tm, LANES), row),
        _const_spec((1, LANES)), _const_spec((1, LANES)),
    ]
    return pl.pallas_call(
        _inproj_kernel, grid=grid, in_specs=in_specs, out_specs=out_specs, out_shape=out_shapes,
        compiler_params=pltpu.CompilerParams(
            dimension_semantics=("parallel", "parallel"), vmem_limit_bytes=VMEM_LIMIT),
        name="inproj",
    )(x, g, w, *tabs, qg, kg)


def _flash_attention(k_tile, vt_tile, qs, s_ref, acc_ref, nkv):
    n = len(qs)

    def scores(j, slot):
        k = k_tile(j)
        out = []
        for c in range(n):
            s = jnp.dot(k, qs[c], preferred_element_type=F32)
            s_ref[c, slot] = s
            out.append(jnp.max(s, axis=0, keepdims=True))
        return tuple(out)

    def update(j, slot, mx, m):
        out = []
        for c in range(n):
            m_new = jnp.maximum(m[c], mx[c])
            alpha = jnp.exp(m[c] - m_new)
            p = jnp.exp(s_ref[c, slot] - m_new).astype(BF16)
            acc_ref[c] = alpha * acc_ref[c] + jnp.dot(vt_tile(j, c), p, preferred_element_type=F32)
            out.append(m_new)
        return tuple(out)

    acc_ref[...] = jnp.zeros_like(acc_ref)
    tq = qs[0].shape[1]
    m0 = (jnp.full((1, tq), -jnp.inf, F32),) * n

    def body(i, carry):
        m, mx_even = carry
        j = 2 * i
        mx_odd = scores(j + 1, 1)
        m = update(j, 0, mx_even, m)
        mx_even = scores(jnp.minimum(j + 2, nkv - 1), 0)
        m = update(j + 1, 1, mx_odd, m)
        return m, mx_even

    lax.fori_loop(0, nkv // 2, body, (m0, scores(0, 0)))


def _attn_a_kernel(lam_ref, sg_ref, qt_ref, k_ref, vt_ref, o_ref, s_ref, acc_ref):
    tq = qt_ref.shape[3]
    nkv, tk = vt_ref.shape[2], vt_ref.shape[4]
    qt = qt_ref[0, 0]
    row = lax.broadcasted_iota(jnp.int32, (LANES, tq), 0)
    zero = jnp.zeros_like(qt)
    qs = (jnp.where(row < HALF, qt, zero), jnp.where(row >= HALF, qt, zero))
    _flash_attention(
        lambda j: k_ref[0, 0, pl.ds(pl.multiple_of(j * tk, tk), tk), :],
        lambda j, c: vt_ref[0, 0, j],
        qs, s_ref, acc_ref, nkv)
    lam_v = lam_ref[...]
    lam = (jnp.exp(jnp.sum(lam_v[0:1] * lam_v[1:2], axis=1, keepdims=True))
           - jnp.exp(jnp.sum(lam_v[2:3] * lam_v[3:4], axis=1, keepdims=True)) + LAM_INIT)
    o1 = acc_ref[0, :A_V_DIM] / acc_ref[0, A_V_DIM:A_V_DIM + 1]
    o2 = acc_ref[1, :A_V_DIM] / acc_ref[1, A_V_DIM:A_V_DIM + 1]
    o = o1 - lam * o2
    y = _rms_rows(o.T, sg_ref[...]) * (1.0 - LAM_INIT)
    o_ref[0] = y.astype(BF16)


def _attn_a(lam4, sg, qat, ka, vat, *, tq):
    b, _, _, s = qat.shape
    nkv, tk = vat.shape[2], vat.shape[4]
    assert nkv % 2 == 0
    return pl.pallas_call(
        _attn_a_kernel,
        grid=(b, A_HEADS, s // tq),
        in_specs=[
            _const_spec((4, A_HEAD_DIM)), _const_spec((1, A_V_DIM)),
            pl.BlockSpec((1, 1, LANES, tq), lambda bi, h, i: (bi, h, 0, i)),
            pl.BlockSpec((1, 1, s, LANES), lambda bi, h, i: (bi, h, 0, 0)),
            pl.BlockSpec((1, 1, nkv, VA_ROWS, tk), lambda bi, h, i: (bi, h, 0, 0, 0)),
        ],
        out_specs=pl.BlockSpec((1, tq, A_V_DIM), lambda bi, h, i: (bi, i, h)),
        out_shape=jax.ShapeDtypeStruct((b, s, A_HEADS * A_V_DIM), BF16),
        scratch_shapes=[pltpu.VMEM((2, 2, tk, tq), F32), pltpu.VMEM((2, VA_ROWS, tq), F32)],
        compiler_params=pltpu.CompilerParams(
            dimension_semantics=("parallel", "parallel", "arbitrary"), vmem_limit_bytes=VMEM_LIMIT),
        name="attn_a",
    )(lam4, sg, qat, ka, vat)


def _attn_b_kernel(qt_ref, k_ref, vt_ref, o_ref, s_ref, acc_ref):
    tq = qt_ref.shape[4]
    nkv, tk = vt_ref.shape[2], vt_ref.shape[4]
    nh = 2 * B_GROUP
    dv = B_HEAD_DIM
    row = lax.broadcasted_iota(jnp.int32, (LANES, tq), 0)
    qs = []
    for half in range(2):
        for g in range(B_GROUP):
            qt = qt_ref[0, 0, g]
            keep = (row < HALF) if half == 0 else (row >= HALF)
            qs.append(jnp.where(keep, qt, jnp.zeros_like(qt)))
    rows = dv + ONES_ROWS

    def vt_tile(j, c):
        half = c // B_GROUP
        return vt_ref[0, 0, j, half * rows:(half + 1) * rows, :]

    _flash_attention(
        lambda j: k_ref[0, 0, pl.ds(pl.multiple_of(j * tk, tk), tk), :],
        vt_tile, qs, s_ref, acc_ref, nkv)
    for blk in range(nh // 2):
        parts = [acc_ref[c, :dv] / acc_ref[c, dv:dv + 1] for c in (2 * blk, 2 * blk + 1)]
        o_ref[0, :, blk * LANES:(blk + 1) * LANES] = jnp.concatenate(parts, axis=0).T.astype(BF16)


def _attn_b(qbt, kb, vbt, *, tq):
    b, _, _, _, s = qbt.shape
    nkv, tk = vbt.shape[2], vbt.shape[4]
    assert nkv % 2 == 0
    width = 2 * B_GROUP * B_HEAD_DIM
    return pl.pallas_call(
        _attn_b_kernel,
        grid=(b, B_PAIRS, s // tq),
        in_specs=[
            pl.BlockSpec((1, 1, B_GROUP, LANES, tq), lambda bi, j, i: (bi, j, 0, 0, i)),
            pl.BlockSpec((1, 1, s, LANES), lambda bi, j, i: (bi, j, 0, 0)),
            pl.BlockSpec((1, 1, nkv, VB_ROWS, tk), lambda bi, j, i: (bi, j, 0, 0, 0)),
        ],
        out_specs=pl.BlockSpec((1, tq, width), lambda bi, j, i: (bi, i, j)),
        out_shape=jax.ShapeDtypeStruct((b, s, B_HEADS * B_HEAD_DIM), BF16),
        scratch_shapes=[pltpu.VMEM((2 * B_GROUP, 2, tk, tq), F32),
                        pltpu.VMEM((2 * B_GROUP, B_HEAD_DIM + ONES_ROWS, tq), F32)],
        compiler_params=pltpu.CompilerParams(
            dimension_semantics=("parallel", "parallel", "arbitrary"), vmem_limit_bytes=VMEM_LIMIT),
        name="attn_b",
    )(qbt, kb, vbt)


def _merge_kernel(x_ref, oa_ref, ob_ref, sig_ref, wa_ref, wb_ref, wo_ref, o_ref):
    pa = jnp.dot(oa_ref[0], wa_ref[...], preferred_element_type=F32)
    pb = jnp.dot(ob_ref[0], wb_ref[...], preferred_element_type=F32)
    mix = sig_ref[0, :, :D_MODEL] * pa + sig_ref[0, :, D_MODEL:] * pb
    o_ref[0] = x_ref[0] + jnp.dot(mix.astype(BF16), wo_ref[...], preferred_element_type=F32)


def _merge(x, oa, ob, sig, wa, wb, wo, *, tm):
    b, s, _ = x.shape
    rows = lambda width: pl.BlockSpec((1, tm, width), lambda bi, i: (bi, i, 0))
    wspec = _const_spec((D_MODEL, D_MODEL))
    return pl.pallas_call(
        _merge_kernel,
        grid=(b, s // tm),
        in_specs=[rows(D_MODEL), rows(D_MODEL), rows(D_MODEL), rows(GATE_W), wspec, wspec, wspec],
        out_specs=rows(D_MODEL),
        out_shape=jax.ShapeDtypeStruct((b, s, D_MODEL), F32),
        compiler_params=pltpu.CompilerParams(
            dimension_semantics=("parallel", "parallel"), vmem_limit_bytes=VMEM_LIMIT),
        name="merge",
    )(x, oa, ob, sig, wa, wb, wo)


def _ffn_kernel(xp_ref, x_ref, xn_ref, g_ref, wu_ref, cw_ref, cb_ref, wd_ref, gf_ref, o_ref, h_ref, *, ck):
    tm = x_ref.shape[1]
    i = pl.program_id(1)
    g = g_ref[...]
    x = x_ref[0]
    hp = jnp.where(i > 0, _rms_rows(xp_ref[0], g), 0.0)
    hn = jnp.where(i < pl.num_programs(1) - 1, _rms_rows(xn_ref[0], g), 0.0)
    h_ref[0:HALO] = hp.astype(BF16)
    h_ref[HALO:HALO + tm] = _rms_rows(x, g).astype(BF16)
    h_ref[HALO + tm:] = hn.astype(BF16)
    h = h_ref[...]

    def conv(u, c0):
        w = cw_ref[:, c0:c0 + ck]
        return (u[HALO - 1:HALO - 1 + tm] * w[0:1] + u[HALO:HALO + tm] * w[1:2]
                + u[HALO + 1:HALO + 1 + tm] * w[2:3] + cb_ref[:, c0:c0 + ck])

    acc = jnp.zeros((tm, D_MODEL), F32)
    for c in range(D_FF // ck):
        val = conv(jnp.dot(h, wu_ref[:, c * ck:(c + 1) * ck], preferred_element_type=F32), c * ck)
        gate = conv(jnp.dot(h, wu_ref[:, D_FF + c * ck:D_FF + (c + 1) * ck], preferred_element_type=F32),
                    D_FF + c * ck)
        act = (jax.nn.silu(gate) * val).astype(BF16)
        acc = acc + jnp.dot(act, wd_ref[c * ck:(c + 1) * ck, :], preferred_element_type=F32)
    o_ref[0] = _rms_rows(x + acc, gf_ref[...])


def _ffn(x, g, wu, cw, cb, wd, gf, *, tm, ck):
    b, s, _ = x.shape
    nh = tm // HALO
    last = s // HALO - 1
    return pl.pallas_call(
        functools.partial(_ffn_kernel, ck=ck),
        grid=(b, s // tm),
        in_specs=[
            pl.BlockSpec((1, HALO, D_MODEL), lambda bi, i: (bi, jnp.maximum(i * nh - 1, 0), 0)),
            pl.BlockSpec((1, tm, D_MODEL), lambda bi, i: (bi, i, 0)),
            pl.BlockSpec((1, HALO, D_MODEL), lambda bi, i: (bi, jnp.minimum((i + 1) * nh, last), 0)),
            _const_spec((1, D_MODEL)),
            _const_spec((D_MODEL, 2 * D_FF)),
            _const_spec((3, 2 * D_FF)),
            _const_spec((1, 2 * D_FF)),
            _const_spec((D_FF, D_MODEL)),
            _const_spec((1, D_MODEL)),
        ],
        out_specs=pl.BlockSpec((1, tm, D_MODEL), lambda bi, i: (bi, i, 0)),
        out_shape=jax.ShapeDtypeStruct((b, s, D_MODEL), F32),
        scratch_shapes=[pltpu.VMEM((tm + 2 * HALO, D_MODEL), BF16)],
        compiler_params=pltpu.CompilerParams(
            dimension_semantics=("parallel", "parallel"), vmem_limit_bytes=VMEM_LIMIT),
        name="ffn",
    )(x, x, x, g, wu, cw, cb, wd, gf)


def _rope_angles(pos, dim, theta):
    inv = 1.0 / (theta ** (jnp.arange(0, dim, 2, dtype=F32) / dim))
    ang = pos.astype(F32)[:, None] * inv[None, :]
    return jnp.cos(ang), jnp.sin(ang)


def _rope_tables(s):
    pos = jnp.arange(s)
    ca, sa = _rope_angles(pos, A_ROT_DIM, A_ROPE_THETA)
    ones = jnp.ones((s, A_HEAD_DIM - A_ROT_DIM), F32)
    cos_a = jnp.concatenate([ca, ca, ones], axis=1)
    sin_a = jnp.concatenate([-sa, sa, 0.0 * ones], axis=1)
    half = B_HEAD_DIM // 2
    cr, sr = _rope_angles(pos // GRID_W, half, B_ROPE_THETA)
    cc, sc = _rope_angles(pos % GRID_W, half, B_ROPE_THETA)
    cos_b = jnp.concatenate([cr, cr, cc, cc], axis=1)
    sin_b = jnp.concatenate([-sr, sr, -sc, sc], axis=1)
    two = lambda t: jnp.concatenate([t, t], axis=1)
    return two(cos_a), two(sin_a), two(cos_b), two(sin_b)


def _w_in_columns():
    cols = list(range(C_QB))
    for j in range(B_PAIRS):
        for g in range(B_GROUP):
            for half in range(2):
                head = (2 * j + half) * B_GROUP + g
                cols.extend(range(C_QB + head * B_HEAD_DIM, C_QB + (head + 1) * B_HEAD_DIM))
    cols.extend(range(C_KB, D_IN))
    return jnp.asarray(cols, dtype=jnp.int32)


def _tile(n, pref):
    return pref if n % pref == 0 else n


def _trunk(x, p, tabs):
    s = x.shape[1]
    tm = _tile(s, 256)
    tk = _tile(s, 512) if s >= 512 else tm
    tm1 = max(tm, tk)
    qat, ka, vat, qbt, kb, vbt, sig = _inproj(x, p["norm_mix_g"], p["w_in"], tabs, p["q_norm_g"], p["k_norm_g"],
                                              tm=tm1, tk=tk)
    oa = _attn_a(p["lam4"], p["subln_g"], qat, ka, vat, tq=tm)
    ob = _attn_b(qbt, kb, vbt, tq=tm)
    x1 = _merge(x, oa, ob, sig, p["w_proj_a"], p["w_proj_b"], p["w_out"], tm=_tile(s, 512))
    return _ffn(x1, p["norm_ffn_g"], p["w_up"], p["conv_w"], p["conv_b"], p["w_down"], p["norm_final_g"],
                tm=_tile(s, 512), ck=256)


def kernel(x_prompt, x_sample, norm_mix_g, w_in, lam_q1, lam_k1, lam_q2, lam_k2, subln_g, q_norm_g, k_norm_g,
           w_proj_a, w_proj_b, w_out, norm_ffn_g, w_up, conv_w, conv_b, w_down, norm_final_g):
    two = lambda t: jnp.concatenate([t, t]).reshape(1, LANES).astype(F32)
    p = {
        "norm_mix_g": norm_mix_g[0].reshape(1, D_MODEL),
        "w_in": jnp.take(w_in[0], _w_in_columns(), axis=1).astype(BF16),
        "lam4": jnp.stack([lam_q1[0], lam_k1[0], lam_q2[0], lam_k2[0]]).astype(F32),
        "subln_g": subln_g[0].reshape(1, A_V_DIM),
        "q_norm_g": two(q_norm_g[0]),
        "k_norm_g": two(k_norm_g[0]),
        "w_proj_a": w_proj_a[0].astype(BF16),
        "w_proj_b": w_proj_b[0].astype(BF16),
        "w_out": w_out[0].astype(BF16),
        "norm_ffn_g": norm_ffn_g[0].reshape(1, D_MODEL),
        "w_up": w_up[0].astype(BF16),
        "conv_w": conv_w[0],
        "conv_b": conv_b[0].reshape(1, 2 * D_FF),
        "w_down": w_down[0].astype(BF16),
        "norm_final_g": norm_final_g.reshape(1, D_MODEL),
    }
    outs = []
    for x in (x_prompt, x_sample):
        outs.append(_trunk(x, p, _rope_tables(x.shape[1])))
    return tuple(outs)
```

```python
import functools
import math

import jax
import jax.numpy as jnp
from jax import lax
from jax.experimental import pallas as pl
from jax.experimental.pallas import tpu as pltpu

D_MODEL = 1024
GRID_W = 64
EPS = 1e-6
A_HEADS = 8
A_HEAD_DIM = 64
A_V_DIM = 128
A_ROT_DIM = 16
A_ROPE_THETA = 500000.0
B_HEADS = 16
B_KV_HEADS = 4
B_GROUP = 4
B_HEAD_DIM = 64
B_ROPE_THETA = 10000.0
D_FF = 2816
LAM_INIT = 0.8 - 0.6 * math.exp(-0.3 * 0)

A_Q = A_HEADS * 2 * A_HEAD_DIM
A_K = A_Q
A_V = A_HEADS * A_V_DIM
B_Q = B_HEADS * B_HEAD_DIM
B_K = B_KV_HEADS * B_HEAD_DIM
B_V = B_K
GATE_W = 2 * D_MODEL
D_IN = A_Q + A_K + A_V + B_Q + B_K + B_V + GATE_W
C_QA = 0
C_KA = C_QA + A_Q
C_VA = C_KA + A_K
C_QB = C_VA + A_V
C_KB = C_QB + B_Q
C_VB = C_KB + B_K
C_GATE = C_VB + B_V

LANES = 128
HALF = 64
B_PAIRS = B_KV_HEADS // 2
HALO = 16
QBLK = 256
ONES_ROWS = 16
VA_ROWS = A_V_DIM + ONES_ROWS
VB_ROWS = 2 * (B_HEAD_DIM + ONES_ROWS)
VMEM_LIMIT = 56 * 1024 * 1024

F32 = jnp.float32
BF16 = jnp.bfloat16


def _rms_rows(x, g):
    return x * lax.rsqrt(jnp.mean(x * x, axis=-1, keepdims=True) + EPS) * g


def _const_spec(shape):
    zeros = (0,) * len(shape)
    return pl.BlockSpec(shape, lambda *_: zeros, pipeline_mode=pl.Buffered(1))


def _inproj_kernel(x_ref, g_ref, w_ref, ca_ref, sa_ref, cb_ref, sb_ref, qg_ref, kg_ref,
                   qat_ref, ka_ref, vat_ref, qbt_ref, kb_ref, vbt_ref, sig_ref):
    tm = x_ref.shape[1]
    h = _rms_rows(x_ref[0], g_ref[...]).astype(BF16)
    lane = lax.broadcasted_iota(jnp.int32, (tm, LANES), 1)
    lo_half = lane < HALF
    first8 = (lane & 8) == 0
    first16 = (lane & 16) == 0
    ca, sa, cb, sb = ca_ref[...], sa_ref[...], cb_ref[...], sb_ref[...]
    qg, kg = qg_ref[...], kg_ref[...]
    scale = A_HEAD_DIM ** -0.5 * math.log2(math.e)

    def proj(c0, n):
        return jnp.dot(h, w_ref[:, c0:c0 + n], preferred_element_type=F32)

    def rope_a(t):
        partner = jnp.where(first8, pltpu.roll(t, LANES - 8, 1), pltpu.roll(t, 8, 1))
        return t * ca + partner * sa

    def rope_b(t):
        partner = jnp.where(first16, pltpu.roll(t, LANES - 16, 1), pltpu.roll(t, 16, 1))
        return t * cb + partner * sb

    def headnorm(t, g):
        sq = t * t
        s_lo = jnp.sum(jnp.where(lo_half, sq, 0.0), axis=1, keepdims=True)
        s_hi = jnp.sum(jnp.where(lo_half, 0.0, sq), axis=1, keepdims=True)
        ms = jnp.where(lo_half, s_lo, s_hi) * (1.0 / HALF)
        return t * lax.rsqrt(ms + EPS) * g

    ones_rows = jnp.ones((ONES_ROWS, tm), BF16)

    def store_vt(ref, idx, t, dv):
        tt = t.T.astype(BF16)
        for grp in range(LANES // dv):
            r0 = grp * (dv + ONES_ROWS)
            ref[idx + (0, slice(r0, r0 + dv))] = tt[grp * dv:(grp + 1) * dv]
            ref[idx + (0, slice(r0 + dv, r0 + dv + ONES_ROWS))] = ones_rows

    for p in range(A_HEADS // 2):
        acc = proj(C_QA + 2 * LANES * p, 2 * LANES)
        for i in range(2):
            t = rope_a(acc[:, i * LANES:(i + 1) * LANES]) * scale
            qat_ref[0, 2 * p + i] = t.T.astype(BF16)
    for p in range(A_HEADS // 2):
        acc = proj(C_KA + 2 * LANES * p, 2 * LANES)
        for i in range(2):
            ka_ref[0, 2 * p + i] = rope_a(acc[:, i * LANES:(i + 1) * LANES]).astype(BF16)
    for p in range(A_HEADS // 2):
        acc = proj(C_VA + 2 * LANES * p, 2 * LANES)
        for i in range(2):
            store_vt(vat_ref, (0, 2 * p + i), acc[:, i * LANES:(i + 1) * LANES], A_V_DIM)
    for p in range(B_PAIRS * B_GROUP // 2):
        acc = proj(C_QB + 2 * LANES * p, 2 * LANES)
        for i in range(2):
            blk = 2 * p + i
            t = rope_b(headnorm(acc[:, i * LANES:(i + 1) * LANES], qg)) * scale
            qbt_ref[0, blk // B_GROUP, blk % B_GROUP] = t.T.astype(BF16)
    acc = proj(C_KB, B_PAIRS * LANES)
    for j in range(B_PAIRS):
        kb_ref[0, j] = rope_b(headnorm(acc[:, j * LANES:(j + 1) * LANES], kg)).astype(BF16)
    acc = proj(C_VB, B_PAIRS * LANES)
    for j in range(B_PAIRS):
        store_vt(vbt_ref, (0, j), acc[:, j * LANES:(j + 1) * LANES], B_HEAD_DIM)
    for p in range(GATE_W // (2 * LANES)):
        acc = proj(C_GATE + 2 * LANES * p, 2 * LANES)
        sig_ref[0, :, 2 * LANES * p:2 * LANES * (p + 1)] = jax.nn.sigmoid(acc)


def _inproj(x, g, w, tabs, qg, kg, *, tm, tk):
    b, s, _ = x.shape
    nt = s // tk
    sub = tk // tm
    grid = (b, s // tm)
    row = lambda bi, i: (i, 0)
    out_shapes = (
        jax.ShapeDtypeStruct((b, A_HEADS, LANES, s), BF16),
        jax.ShapeDtypeStruct((b, A_HEADS, s, LANES), BF16),
        jax.ShapeDtypeStruct((b, A_HEADS, nt, VA_ROWS, tk), BF16),
        jax.ShapeDtypeStruct((b, B_PAIRS, B_GROUP, LANES, s), BF16),
        jax.ShapeDtypeStruct((b, B_PAIRS, s, LANES), BF16),
        jax.ShapeDtypeStruct((b, B_PAIRS, nt, VB_ROWS, tk), BF16),
        jax.ShapeDtypeStruct((b, s, GATE_W), F32),
    )
    out_specs = (
        pl.BlockSpec((1, A_HEADS, LANES, tm), lambda bi, i: (bi, 0, 0, i)),
        pl.BlockSpec((1, A_HEADS, tm, LANES), lambda bi, i: (bi, 0, i, 0)),
        pl.BlockSpec((1, A_HEADS, 1, VA_ROWS, tm), lambda bi, i: (bi, 0, i // sub, 0, i % sub)),
        pl.BlockSpec((1, B_PAIRS, B_GROUP, LANES, tm), lambda bi, i: (bi, 0, 0, 0, i)),
        pl.BlockSpec((1, B_PAIRS, tm, LANES), lambda bi, i: (bi, 0, i, 0)),
        pl.BlockSpec((1, B_PAIRS, 1, VB_ROWS, tm), lambda bi, i: (bi, 0, i // sub, 0, i % sub)),
        pl.BlockSpec((1, tm, GATE_W), lambda bi, i: (bi, i, 0)),
    )
    in_specs = [
        pl.BlockSpec((1, tm, D_MODEL), lambda bi, i: (bi, i, 0)),
        _const_spec((1, D_MODEL)),
        _const_spec((D_MODEL, D_IN)),
        pl.BlockSpec((tm, LANES), row), pl.BlockSpec((tm, LANES), row),
        pl.BlockSpec((tm, LANES), row), pl.BlockSpec((tm, LANES), row),
        _const_spec((1, LANES)), _const_spec((1, LANES)),
    ]
    return pl.pallas_call(
        _inproj_kernel, grid=grid, in_specs=in_specs, out_specs=out_specs, out_shape=out_shapes,
        compiler_params=pltpu.CompilerParams(
            dimension_semantics=("parallel", "parallel"), vmem_limit_bytes=VMEM_LIMIT),
        name="inproj",
    )(x, g, w, *tabs, qg, kg)


def _flash_attention(k_tile, vt_tile, qs, s_ref, acc_ref, nkv, unroll):
    n = len(qs)
    assert unroll % 2 == 0 and nkv % unroll == 0

    def score(k, c, slot):
        s = jnp.dot(k, qs[c], preferred_element_type=F32)
        s_ref[c, slot] = s
        return jnp.max(s, axis=0, keepdims=True)

    def scores(j, slot):
        k = k_tile(j)
        return tuple(score(k, c, slot) for c in range(n))

    def update(j, c, slot, mx, m):
        m_new = jnp.maximum(m, mx)
        alpha = jnp.exp2(m - m_new)
        p = jnp.exp2(s_ref[c, slot] - m_new).astype(BF16)
        acc_ref[c] = alpha * acc_ref[c] + jnp.dot(vt_tile(j, c), p, preferred_element_type=F32)
        return m_new

    acc_ref[...] = jnp.zeros_like(acc_ref)
    tq = qs[0].shape[1]
    m0 = (jnp.full((1, tq), -jnp.inf, F32),) * n

    def tiles(j0, m, mx, last):
        m = list(m)
        for t in range(unroll):
            skip = last and t == unroll - 1
            k = None if skip else k_tile(j0 + t + 1)
            mx_next = []
            for c in range(n):
                if not skip:
                    mx_next.append(score(k, c, (t + 1) % 2))
                m[c] = update(j0 + t, c, t % 2, mx[c], m[c])
            mx = tuple(mx_next)
        return tuple(m), mx

    carry = (m0, scores(0, 0))
    carry = lax.fori_loop(0, nkv // unroll - 1, lambda i, c: tiles(i * unroll, *c, False), carry)
    tiles(nkv - unroll, *carry, True)


def _key_rows(j, tk):
    start = j * tk
    return pl.ds(start if isinstance(start, int) else pl.multiple_of(start, tk), tk)


def _attn_a_kernel(lam_ref, sg_ref, qt_ref, k_ref, vt_ref, o_ref, s_ref, acc_ref, *, unroll):
    tq = qt_ref.shape[3]
    nkv, tk = vt_ref.shape[2], vt_ref.shape[4]
    row = lax.broadcasted_iota(jnp.int32, (LANES, QBLK), 0)
    qs = []
    for blk in range(tq // QBLK):
        qt = qt_ref[0, 0, :, blk * QBLK:(blk + 1) * QBLK]
        zero = jnp.zeros_like(qt)
        qs += [jnp.where(row < HALF, qt, zero), jnp.where(row >= HALF, qt, zero)]
    _flash_attention(
        lambda j: k_ref[0, 0, _key_rows(j, tk), :],
        lambda j, c: vt_ref[0, 0, j],
        qs, s_ref, acc_ref, nkv, unroll)
    lam_v = lam_ref[...]
    lam = (jnp.exp(jnp.sum(lam_v[0:1] * lam_v[1:2], axis=1, keepdims=True))
           - jnp.exp(jnp.sum(lam_v[2:3] * lam_v[3:4], axis=1, keepdims=True)) + LAM_INIT)
    for blk in range(tq // QBLK):
        o1 = acc_ref[2 * blk, :A_V_DIM] / acc_ref[2 * blk, A_V_DIM:A_V_DIM + 1]
        o2 = acc_ref[2 * blk + 1, :A_V_DIM] / acc_ref[2 * blk + 1, A_V_DIM:A_V_DIM + 1]
        o = o1 - lam * o2
        y = _rms_rows(o.T, sg_ref[...]) * (1.0 - LAM_INIT)
        o_ref[0, blk * QBLK:(blk + 1) * QBLK, :] = y.astype(BF16)


def _attn_a(lam4, sg, qat, ka, vat, *, tq, unroll):
    b, _, _, s = qat.shape
    nkv, tk = vat.shape[2], vat.shape[4]
    return pl.pallas_call(
        functools.partial(_attn_a_kernel, unroll=unroll),
        grid=(b, A_HEADS, s // tq),
        in_specs=[
            _const_spec((4, A_HEAD_DIM)), _const_spec((1, A_V_DIM)),
            pl.BlockSpec((1, 1, LANES, tq), lambda bi, h, i: (bi, h, 0, i)),
            pl.BlockSpec((1, 1, s, LANES), lambda bi, h, i: (bi, h, 0, 0)),
            pl.BlockSpec((1, 1, nkv, VA_ROWS, tk), lambda bi, h, i: (bi, h, 0, 0, 0)),
        ],
        out_specs=pl.BlockSpec((1, tq, A_V_DIM), lambda bi, h, i: (bi, i, h)),
        out_shape=jax.ShapeDtypeStruct((b, s, A_HEADS * A_V_DIM), BF16),
        scratch_shapes=[pltpu.VMEM((2 * tq // QBLK, 2, tk, QBLK), F32),
                        pltpu.VMEM((2 * tq // QBLK, VA_ROWS, QBLK), F32)],
        compiler_params=pltpu.CompilerParams(
            dimension_semantics=("parallel", "parallel", "arbitrary"), vmem_limit_bytes=VMEM_LIMIT),
        name="attn_a",
    )(lam4, sg, qat, ka, vat)


def _attn_b_kernel(qt_ref, k_ref, vt_ref, o_ref, s_ref, acc_ref, *, unroll):
    tq = qt_ref.shape[4]
    nkv, tk = vt_ref.shape[2], vt_ref.shape[4]
    nh = 2 * B_GROUP
    dv = B_HEAD_DIM
    row = lax.broadcasted_iota(jnp.int32, (LANES, tq), 0)
    first = pl.program_id(2) * HALF
    keep = (row >= first) & (row < first + HALF)
    qs = []
    for g in range(B_GROUP):
        qt = qt_ref[0, 0, g]
        qs.append(jnp.where(keep, qt, jnp.zeros_like(qt)))

    _flash_attention(lambda j: k_ref[0, 0, _key_rows(j, tk), :], lambda j, c: vt_ref[0, 0, j],
                     qs, s_ref, acc_ref, nkv, unroll)
    for blk in range(B_GROUP // 2):
        parts = [acc_ref[c, :dv] / acc_ref[c, dv:dv + 1] for c in (2 * blk, 2 * blk + 1)]
        o_ref[0, :, blk * LANES:(blk + 1) * LANES] = jnp.concatenate(parts, axis=0).T.astype(BF16)


def _attn_b(qbt, kb, vbt, *, unroll):
    b, _, _, _, s = qbt.shape
    nkv, tk = vbt.shape[2], vbt.shape[4]
    tq = QBLK
    width = B_GROUP * B_HEAD_DIM
    rows = B_HEAD_DIM + ONES_ROWS
    return pl.pallas_call(
        functools.partial(_attn_b_kernel, unroll=unroll),
        grid=(b, B_PAIRS, 2, s // tq),
        in_specs=[
            pl.BlockSpec((1, 1, B_GROUP, LANES, tq), lambda bi, j, hf, i: (bi, j, 0, 0, i)),
            pl.BlockSpec((1, 1, s, LANES), lambda bi, j, hf, i: (bi, j, 0, 0)),
            pl.BlockSpec((1, 1, nkv, rows, tk), lambda bi, j, hf, i: (bi, j, 0, hf, 0)),
        ],
        out_specs=pl.BlockSpec((1, tq, width), lambda bi, j, hf, i: (bi, i, 2 * j + hf)),
        out_shape=jax.ShapeDtypeStruct((b, s, B_HEADS * B_HEAD_DIM), BF16),
        scratch_shapes=[pltpu.VMEM((B_GROUP, 2, tk, tq), F32), pltpu.VMEM((B_GROUP, rows, tq), F32)],
        compiler_params=pltpu.CompilerParams(
            dimension_semantics=("parallel", "parallel", "parallel", "arbitrary"), vmem_limit_bytes=VMEM_LIMIT),
        name="attn_b",
    )(qbt, kb, vbt)


def _merge_kernel(x_ref, oa_ref, ob_ref, sig_ref, wa_ref, wb_ref, wo_ref, o_ref):
    pa = jnp.dot(oa_ref[0], wa_ref[...], preferred_element_type=F32)
    pb = jnp.dot(ob_ref[0], wb_ref[...], preferred_element_type=F32)
    mix = sig_ref[0, :, :D_MODEL] * pa + sig_ref[0, :, D_MODEL:] * pb
    o_ref[0] = x_ref[0] + jnp.dot(mix.astype(BF16), wo_ref[...], preferred_element_type=F32)


def _merge(x, oa, ob, sig, wa, wb, wo, *, tm):
    b, s, _ = x.shape
    rows = lambda width: pl.BlockSpec((1, tm, width), lambda bi, i: (bi, i, 0))
    wspec = _const_spec((D_MODEL, D_MODEL))
    return pl.pallas_call(
        _merge_kernel,
        grid=(b, s // tm),
        in_specs=[rows(D_MODEL), rows(D_MODEL), rows(D_MODEL), rows(GATE_W), wspec, wspec, wspec],
        out_specs=rows(D_MODEL),
        out_shape=jax.ShapeDtypeStruct((b, s, D_MODEL), F32),
        compiler_params=pltpu.CompilerParams(
            dimension_semantics=("parallel", "parallel"), vmem_limit_bytes=VMEM_LIMIT),
        name="merge",
    )(x, oa, ob, sig, wa, wb, wo)


def _ffn_kernel(xp_ref, x_ref, xn_ref, g_ref, wu_ref, cw_ref, cb_ref, wd_ref, gf_ref, o_ref, h_ref, *, ck):
    tm = x_ref.shape[1]
    i = pl.program_id(1)
    g = g_ref[...]
    x = x_ref[0]
    hp = jnp.where(i > 0, _rms_rows(xp_ref[0], g), 0.0)
    hn = jnp.where(i < pl.num_programs(1) - 1, _rms_rows(xn_ref[0], g), 0.0)
    h_ref[0:HALO] = hp.astype(BF16)
    h_ref[HALO:HALO + tm] = _rms_rows(x, g).astype(BF16)
    h_ref[HALO + tm:] = hn.astype(BF16)
    h = h_ref[...]

    def conv(u, c0):
        w = cw_ref[:, c0:c0 + ck]
        return (u[HALO - 1:HALO - 1 + tm] * w[0:1] + u[HALO:HALO + tm] * w[1:2]
                + u[HALO + 1:HALO + 1 + tm] * w[2:3] + cb_ref[:, c0:c0 + ck])

    acc = jnp.zeros((tm, D_MODEL), F32)
    for c in range(D_FF // ck):
        val = conv(jnp.dot(h, wu_ref[:, c * ck:(c + 1) * ck], preferred_element_type=F32), c * ck)
        gate = conv(jnp.dot(h, wu_ref[:, D_FF + c * ck:D_FF + (c + 1) * ck], preferred_element_type=F32),
                    D_FF + c * ck)
        act = (jax.nn.silu(gate) * val).astype(BF16)
        acc = acc + jnp.dot(act, wd_ref[c * ck:(c + 1) * ck, :], preferred_element_type=F32)
    o_ref[0] = _rms_rows(x + acc, gf_ref[...])


def _ffn(x, g, wu, cw, cb, wd, gf, *, tm, ck):
    b, s, _ = x.shape
    nh = tm // HALO
    last = s // HALO - 1
    return pl.pallas_call(
        functools.partial(_ffn_kernel, ck=ck),
        grid=(b, s // tm),
        in_specs=[
            pl.BlockSpec((1, HALO, D_MODEL), lambda bi, i: (bi, jnp.maximum(i * nh - 1, 0), 0)),
            pl.BlockSpec((1, tm, D_MODEL), lambda bi, i: (bi, i, 0)),
            pl.BlockSpec((1, HALO, D_MODEL), lambda bi, i: (bi, jnp.minimum((i + 1) * nh, last), 0)),
            _const_spec((1, D_MODEL)),
            _const_spec((D_MODEL, 2 * D_FF)),
            _const_spec((3, 2 * D_FF)),
            _const_spec((1, 2 * D_FF)),
            _const_spec((D_FF, D_MODEL)),
            _const_spec((1, D_MODEL)),
        ],
        out_specs=pl.BlockSpec((1, tm, D_MODEL), lambda bi, i: (bi, i, 0)),
        out_shape=jax.ShapeDtypeStruct((b, s, D_MODEL), F32),
        scratch_shapes=[pltpu.VMEM((tm + 2 * HALO, D_MODEL), BF16)],
        compiler_params=pltpu.CompilerParams(
            dimension_semantics=("parallel", "parallel"), vmem_limit_bytes=VMEM_LIMIT),
        name="ffn",
    )(x, x, x, g, wu, cw, cb, wd, gf)


def _rope_angles(pos, dim, theta):
    inv = 1.0 / (theta ** (jnp.arange(0, dim, 2, dtype=F32) / dim))
    ang = pos.astype(F32)[:, None] * inv[None, :]
    return jnp.cos(ang), jnp.sin(ang)


def _rope_tables(s):
    pos = jnp.arange(s)
    ca, sa = _rope_angles(pos, A_ROT_DIM, A_ROPE_THETA)
    ones = jnp.ones((s, A_HEAD_DIM - A_ROT_DIM), F32)
    cos_a = jnp.concatenate([ca, ca, ones], axis=1)
    sin_a = jnp.concatenate([-sa, sa, 0.0 * ones], axis=1)
    half = B_HEAD_DIM // 2
    cr, sr = _rope_angles(pos // GRID_W, half, B_ROPE_THETA)
    cc, sc = _rope_angles(pos % GRID_W, half, B_ROPE_THETA)
    cos_b = jnp.concatenate([cr, cr, cc, cc], axis=1)
    sin_b = jnp.concatenate([-sr, sr, -sc, sc], axis=1)
    two = lambda t: jnp.concatenate([t, t], axis=1)
    return two(cos_a), two(sin_a), two(cos_b), two(sin_b)


def _w_in_columns():
    cols = list(range(C_QB))
    for j in range(B_PAIRS):
        for g in range(B_GROUP):
            for half in range(2):
                head = (2 * j + half) * B_GROUP + g
                cols.extend(range(C_QB + head * B_HEAD_DIM, C_QB + (head + 1) * B_HEAD_DIM))
    cols.extend(range(C_KB, D_IN))
    return jnp.asarray(cols, dtype=jnp.int32)


def _tile(n, pref):
    return pref if n % pref == 0 else n


def _trunk(x, p, tabs):
    s = x.shape[1]
    tk = 512
    nkv = s // tk
    qat, ka, vat, qbt, kb, vbt, sig = _inproj(x, p["norm_mix_g"], p["w_in"], tabs, p["q_norm_g"], p["k_norm_g"],
                                              tm=512, tk=tk)
    unroll = 4 if nkv % 4 == 0 else 2
    oa = _attn_a(p["lam4"], p["subln_g"], qat, ka, vat, tq=2 * QBLK, unroll=unroll)
    ob = _attn_b(qbt, kb, vbt, unroll=unroll)
    x1 = _merge(x, oa, ob, sig, p["w_proj_a"], p["w_proj_b"], p["w_out"], tm=_tile(s, 512))
    return _ffn(x1, p["norm_ffn_g"], p["w_up"], p["conv_w"], p["conv_b"], p["w_down"], p["norm_final_g"],
                tm=_tile(s, 512), ck=256)


def kernel(x_prompt, x_sample, norm_mix_g, w_in, lam_q1, lam_k1, lam_q2, lam_k2, subln_g, q_norm_g, k_norm_g,
           w_proj_a, w_proj_b, w_out, norm_ffn_g, w_up, conv_w, conv_b, w_down, norm_final_g):
    two = lambda t: jnp.concatenate([t, t]).reshape(1, LANES).astype(F32)
    p = {
        "norm_mix_g": norm_mix_g[0].reshape(1, D_MODEL),
        "w_in": jnp.take(w_in[0], _w_in_columns(), axis=1).astype(BF16),
        "lam4": jnp.stack([lam_q1[0], lam_k1[0], lam_q2[0], lam_k2[0]]).astype(F32),
        "subln_g": subln_g[0].reshape(1, A_V_DIM),
        "q_norm_g": two(q_norm_g[0]),
        "k_norm_g": two(k_norm_g[0]),
        "w_proj_a": w_proj_a[0].astype(BF16),
        "w_proj_b": w_proj_b[0].astype(BF16),
        "w_out": w_out[0].astype(BF16),
        "norm_ffn_g": norm_ffn_g[0].reshape(1, D_MODEL),
        "w_up": w_up[0].astype(BF16),
        "conv_w": conv_w[0],
        "conv_b": conv_b[0].reshape(1, 2 * D_FF),
        "w_down": w_down[0].astype(BF16),
        "norm_final_g": norm_final_g.reshape(1, D_MODEL),
    }
    outs = []
    for x in (x_prompt, x_sample):
        outs.append(_trunk(x, p, _rope_tables(x.shape[1])))
    return tuple(outs)
```

```python
import functools
import math

import jax
import jax.numpy as jnp
from jax import lax
from jax.experimental import pallas as pl
from jax.experimental.pallas import tpu as pltpu

D_MODEL = 1024
GRID_W = 64
EPS = 1e-6
A_HEADS = 8
A_HEAD_DIM = 64
A_V_DIM = 128
A_ROT_DIM = 16
A_ROPE_THETA = 500000.0
B_HEADS = 16
B_KV_HEADS = 4
B_GROUP = 4
B_HEAD_DIM = 64
B_ROPE_THETA = 10000.0
D_FF = 2816
LAM_INIT = 0.8 - 0.6 * math.exp(-0.3 * 0)

A_Q = A_HEADS * 2 * A_HEAD_DIM
A_K = A_Q
A_V = A_HEADS * A_V_DIM
B_Q = B_HEADS * B_HEAD_DIM
B_K = B_KV_HEADS * B_HEAD_DIM
B_V = B_K
GATE_W = 2 * D_MODEL
D_IN = A_Q + A_K + A_V + B_Q + B_K + B_V + GATE_W
C_QA = 0
C_KA = C_QA + A_Q
C_VA = C_KA + A_K
C_QB = C_VA + A_V
C_KB = C_QB + B_Q
C_VB = C_KB + B_K
C_GATE = C_VB + B_V

LANES = 128
HALF = 64
B_PAIRS = B_KV_HEADS // 2
HALO = 16
QBLK = 256
A_GROUP_Q = 2 * QBLK
ONES_ROWS = 16
VA_ROWS = A_V_DIM + ONES_ROWS
VB_ROWS = 2 * (B_HEAD_DIM + ONES_ROWS)
VMEM_LIMIT = 56 * 1024 * 1024

F32 = jnp.float32
BF16 = jnp.bfloat16


def _rms_rows(x, g):
    return x * lax.rsqrt(jnp.mean(x * x, axis=-1, keepdims=True) + EPS) * g


def _const_spec(shape):
    zeros = (0,) * len(shape)
    return pl.BlockSpec(shape, lambda *_: zeros, pipeline_mode=pl.Buffered(1))


def _inproj_kernel(x_ref, g_ref, w_ref, ca_ref, sa_ref, cb_ref, sb_ref, qg_ref, kg_ref,
                   qat_ref, ka_ref, vat_ref, qbt_ref, kb_ref, vbt_ref, sig_ref):
    tm = x_ref.shape[1]
    h = _rms_rows(x_ref[0], g_ref[...]).astype(BF16)
    lane = lax.broadcasted_iota(jnp.int32, (tm, LANES), 1)
    lo_half = lane < HALF
    first8 = (lane & 8) == 0
    first16 = (lane & 16) == 0
    ca, sa, cb, sb = ca_ref[...], sa_ref[...], cb_ref[...], sb_ref[...]
    qg, kg = qg_ref[...], kg_ref[...]
    scale = A_HEAD_DIM ** -0.5 * math.log2(math.e)

    def proj(c0, n):
        return jnp.dot(h, w_ref[:, c0:c0 + n], preferred_element_type=F32)

    def rope_a(t):
        partner = jnp.where(first8, pltpu.roll(t, LANES - 8, 1), pltpu.roll(t, 8, 1))
        return t * ca + partner * sa

    def rope_b(t):
        partner = jnp.where(first16, pltpu.roll(t, LANES - 16, 1), pltpu.roll(t, 16, 1))
        return t * cb + partner * sb

    def headnorm(t, g):
        sq = t * t
        s_lo = jnp.sum(jnp.where(lo_half, sq, 0.0), axis=1, keepdims=True)
        s_hi = jnp.sum(jnp.where(lo_half, 0.0, sq), axis=1, keepdims=True)
        ms = jnp.where(lo_half, s_lo, s_hi) * (1.0 / HALF)
        return t * lax.rsqrt(ms + EPS) * g

    ones_rows = jnp.ones((ONES_ROWS, tm), BF16)

    def store_vt(ref, idx, t, dv):
        tt = t.T.astype(BF16)
        for grp in range(LANES // dv):
            r0 = grp * (dv + ONES_ROWS)
            ref[idx + (0, slice(r0, r0 + dv))] = tt[grp * dv:(grp + 1) * dv]
            ref[idx + (0, slice(r0 + dv, r0 + dv + ONES_ROWS))] = ones_rows

    for p in range(A_HEADS // 2):
        acc = proj(C_QA + 2 * LANES * p, 2 * LANES)
        for i in range(2):
            t = rope_a(acc[:, i * LANES:(i + 1) * LANES]) * scale
            qat_ref[0, 2 * p + i] = t.T.astype(BF16)
    for p in range(A_HEADS // 2):
        acc = proj(C_KA + 2 * LANES * p, 2 * LANES)
        for i in range(2):
            ka_ref[0, 2 * p + i] = rope_a(acc[:, i * LANES:(i + 1) * LANES]).astype(BF16)
    for p in range(A_HEADS // 2):
        acc = proj(C_VA + 2 * LANES * p, 2 * LANES)
        for i in range(2):
            store_vt(vat_ref, (0, 2 * p + i), acc[:, i * LANES:(i + 1) * LANES], A_V_DIM)
    for p in range(B_PAIRS * B_GROUP // 2):
        acc = proj(C_QB + 2 * LANES * p, 2 * LANES)
        for i in range(2):
            blk = 2 * p + i
            t = rope_b(headnorm(acc[:, i * LANES:(i + 1) * LANES], qg)) * scale
            qbt_ref[0, blk // B_GROUP, blk % B_GROUP] = t.T.astype(BF16)
    acc = proj(C_KB, B_PAIRS * LANES)
    for j in range(B_PAIRS):
        kb_ref[0, j] = rope_b(headnorm(acc[:, j * LANES:(j + 1) * LANES], kg)).astype(BF16)
    acc = proj(C_VB, B_PAIRS * LANES)
    for j in range(B_PAIRS):
        store_vt(vbt_ref, (0, j), acc[:, j * LANES:(j + 1) * LANES], B_HEAD_DIM)
    for p in range(GATE_W // (2 * LANES)):
        acc = proj(C_GATE + 2 * LANES * p, 2 * LANES)
        sig_ref[0, :, 2 * LANES * p:2 * LANES * (p + 1)] = jax.nn.sigmoid(acc)


def _inproj(x, g, w, tabs, qg, kg, *, tm, tk):
    b, s, _ = x.shape
    nt = s // tk
    sub = tk // tm
    grid = (b, s // tm)
    row = lambda bi, i: (i, 0)
    out_shapes = (
        jax.ShapeDtypeStruct((b, A_HEADS, LANES, s), BF16),
        jax.ShapeDtypeStruct((b, A_HEADS, s, LANES), BF16),
        jax.ShapeDtypeStruct((b, A_HEADS, nt, VA_ROWS, tk), BF16),
        jax.ShapeDtypeStruct((b, B_PAIRS, B_GROUP, LANES, s), BF16),
        jax.ShapeDtypeStruct((b, B_PAIRS, s, LANES), BF16),
        jax.ShapeDtypeStruct((b, B_PAIRS, nt, VB_ROWS, tk), BF16),
        jax.ShapeDtypeStruct((b, s, GATE_W), F32),
    )
    out_specs = (
        pl.BlockSpec((1, A_HEADS, LANES, tm), lambda bi, i: (bi, 0, 0, i)),
        pl.BlockSpec((1, A_HEADS, tm, LANES), lambda bi, i: (bi, 0, i, 0)),
        pl.BlockSpec((1, A_HEADS, 1, VA_ROWS, tm), lambda bi, i: (bi, 0, i // sub, 0, i % sub)),
        pl.BlockSpec((1, B_PAIRS, B_GROUP, LANES, tm), lambda bi, i: (bi, 0, 0, 0, i)),
        pl.BlockSpec((1, B_PAIRS, tm, LANES), lambda bi, i: (bi, 0, i, 0)),
        pl.BlockSpec((1, B_PAIRS, 1, VB_ROWS, tm), lambda bi, i: (bi, 0, i // sub, 0, i % sub)),
        pl.BlockSpec((1, tm, GATE_W), lambda bi, i: (bi, i, 0)),
    )
    in_specs = [
        pl.BlockSpec((1, tm, D_MODEL), lambda bi, i: (bi, i, 0)),
        _const_spec((1, D_MODEL)),
        _const_spec((D_MODEL, D_IN)),
        pl.BlockSpec((tm, LANES), row), pl.BlockSpec((tm, LANES), row),
        pl.BlockSpec((tm, LANES), row), pl.BlockSpec((tm, LANES), row),
        _const_spec((1, LANES)), _const_spec((1, LANES)),
    ]
    return pl.pallas_call(
        _inproj_kernel, grid=grid, in_specs=in_specs, out_specs=out_specs, out_shape=out_shapes,
        compiler_params=pltpu.CompilerParams(
            dimension_semantics=("parallel", "parallel"), vmem_limit_bytes=VMEM_LIMIT),
        name="inproj",
    )(x, g, w, *tabs, qg, kg)


def _flash_attention(k_tile, vt_tile, qs, s_ref, acc_ref, nkv, unroll):
    n = len(qs)
    assert unroll % 2 == 0 and nkv % unroll == 0

    def score(k, c, slot):
        s = jnp.dot(k, qs[c], preferred_element_type=F32)
        s_ref[c, slot] = s
        return jnp.max(s, axis=0, keepdims=True)

    def scores(j, slot):
        k = k_tile(j)
        return tuple(score(k, c, slot) for c in range(n))

    def update(j, c, slot, mx, m):
        m_new = jnp.maximum(m, mx)
        alpha = jnp.exp2(m - m_new)
        p = jnp.exp2(s_ref[c, slot] - m_new).astype(BF16)
        acc_ref[c] = alpha * acc_ref[c] + jnp.dot(vt_tile(j, c), p, preferred_element_type=F32)
        return m_new

    acc_ref[...] = jnp.zeros(acc_ref.shape, F32)
    tq = qs[0].shape[1]
    m0 = (jnp.full((1, tq), -jnp.inf, F32),) * n

    def tiles(j0, m, mx, last):
        m = list(m)
        for t in range(unroll):
            skip = last and t == unroll - 1
            k = None if skip else k_tile(j0 + t + 1)
            mx_next = []
            for c in range(n):
                if not skip:
                    mx_next.append(score(k, c, (t + 1) % 2))
                m[c] = update(j0 + t, c, t % 2, mx[c], m[c])
            mx = tuple(mx_next)
        return tuple(m), mx

    carry = (m0, scores(0, 0))
    carry = lax.fori_loop(0, nkv // unroll - 1, lambda i, c: tiles(i * unroll, *c, False), carry)
    tiles(nkv - unroll, *carry, True)


def _key_rows(j, tk):
    start = j * tk
    return pl.ds(start if isinstance(start, int) else pl.multiple_of(start, tk), tk)


def _attn_a_kernel(lam_ref, sg_ref, qt_ref, k_ref, vt_ref, o_ref, s_ref, acc_ref, *, unroll):
    tq = qt_ref.shape[3]
    nkv, tk = vt_ref.shape[2], vt_ref.shape[4]
    row = lax.broadcasted_iota(jnp.int32, (LANES, QBLK), 0)
    lam_v = lam_ref[...]
    lam = (jnp.exp(jnp.sum(lam_v[0:1] * lam_v[1:2], axis=1, keepdims=True))
           - jnp.exp(jnp.sum(lam_v[2:3] * lam_v[3:4], axis=1, keepdims=True)) + LAM_INIT)
    for grp in range(tq // A_GROUP_Q):
        acc = acc_ref.at[grp]
        blocks = [grp * A_GROUP_Q + i * QBLK for i in range(A_GROUP_Q // QBLK)]
        qs = []
        for q0 in blocks:
            qt = qt_ref[0, 0, :, q0:q0 + QBLK]
            zero = jnp.zeros_like(qt)
            qs += [jnp.where(row < HALF, qt, zero), jnp.where(row >= HALF, qt, zero)]
        _flash_attention(
            lambda j: k_ref[0, 0, _key_rows(j, tk), :],
            lambda j, c: vt_ref[0, 0, j],
            qs, s_ref, acc, nkv, unroll)
        for i, q0 in enumerate(blocks):
            o1 = acc[2 * i, :A_V_DIM] / acc[2 * i, A_V_DIM:A_V_DIM + 1]
            o2 = acc[2 * i + 1, :A_V_DIM] / acc[2 * i + 1, A_V_DIM:A_V_DIM + 1]
            o = o1 - lam * o2
            y = _rms_rows(o.T, sg_ref[...]) * (1.0 - LAM_INIT)
            o_ref[0, q0:q0 + QBLK, :] = y.astype(BF16)


def _attn_a(lam4, sg, qat, ka, vat, *, tq, unroll):
    b, _, _, s = qat.shape
    nkv, tk = vat.shape[2], vat.shape[4]
    chains = 2 * A_GROUP_Q // QBLK
    return pl.pallas_call(
        functools.partial(_attn_a_kernel, unroll=unroll),
        grid=(b, A_HEADS, s // tq),
        in_specs=[
            _const_spec((4, A_HEAD_DIM)), _const_spec((1, A_V_DIM)),
            pl.BlockSpec((1, 1, LANES, tq), lambda bi, h, i: (bi, h, 0, i)),
            pl.BlockSpec((1, 1, s, LANES), lambda bi, h, i: (bi, h, 0, 0)),
            pl.BlockSpec((1, 1, nkv, VA_ROWS, tk), lambda bi, h, i: (bi, h, 0, 0, 0)),
        ],
        out_specs=pl.BlockSpec((1, tq, A_V_DIM), lambda bi, h, i: (bi, i, h)),
        out_shape=jax.ShapeDtypeStruct((b, s, A_HEADS * A_V_DIM), BF16),
        scratch_shapes=[pltpu.VMEM((chains, 2, tk, QBLK), F32),
                        pltpu.VMEM((tq // A_GROUP_Q, chains, VA_ROWS, QBLK), F32)],
        compiler_params=pltpu.CompilerParams(
            dimension_semantics=("parallel", "parallel", "arbitrary"), vmem_limit_bytes=VMEM_LIMIT),
        name="attn_a",
    )(lam4, sg, qat, ka, vat)


def _attn_b_kernel(qt_ref, k_ref, vt_ref, o_ref, s_ref, acc_ref, *, unroll):
    tq = qt_ref.shape[4]
    nkv, tk = vt_ref.shape[2], vt_ref.shape[4]
    dv = B_HEAD_DIM
    row = lax.broadcasted_iota(jnp.int32, (LANES, QBLK), 0)
    first = pl.program_id(2) * HALF
    keep = (row >= first) & (row < first + HALF)
    for grp in range(tq // QBLK):
        acc = acc_ref.at[grp]
        q0 = grp * QBLK
        qs = []
        for g in range(B_GROUP):
            qt = qt_ref[0, 0, g, :, q0:q0 + QBLK]
            qs.append(jnp.where(keep, qt, jnp.zeros_like(qt)))
        _flash_attention(lambda j: k_ref[0, 0, _key_rows(j, tk), :], lambda j, c: vt_ref[0, 0, j],
                         qs, s_ref, acc, nkv, unroll)
        for blk in range(B_GROUP // 2):
            parts = [acc[c, :dv] / acc[c, dv:dv + 1] for c in (2 * blk, 2 * blk + 1)]
            o_ref[0, q0:q0 + QBLK, blk * LANES:(blk + 1) * LANES] = (
                jnp.concatenate(parts, axis=0).T.astype(BF16))


def _attn_b(qbt, kb, vbt, *, tq, unroll):
    b, _, _, _, s = qbt.shape
    nkv, tk = vbt.shape[2], vbt.shape[4]
    width = B_GROUP * B_HEAD_DIM
    rows = B_HEAD_DIM + ONES_ROWS
    return pl.pallas_call(
        functools.partial(_attn_b_kernel, unroll=unroll),
        grid=(b, B_PAIRS, 2, s // tq),
        in_specs=[
            pl.BlockSpec((1, 1, B_GROUP, LANES, tq), lambda bi, j, hf, i: (bi, j, 0, 0, i)),
            pl.BlockSpec((1, 1, s, LANES), lambda bi, j, hf, i: (bi, j, 0, 0)),
            pl.BlockSpec((1, 1, nkv, rows, tk), lambda bi, j, hf, i: (bi, j, 0, hf, 0)),
        ],
        out_specs=pl.BlockSpec((1, tq, width), lambda bi, j, hf, i: (bi, i, 2 * j + hf)),
        out_shape=jax.ShapeDtypeStruct((b, s, B_HEADS * B_HEAD_DIM), BF16),
        scratch_shapes=[pltpu.VMEM((B_GROUP, 2, tk, QBLK), F32),
                        pltpu.VMEM((tq // QBLK, B_GROUP, rows, QBLK), F32)],
        compiler_params=pltpu.CompilerParams(
            dimension_semantics=("parallel", "parallel", "parallel", "arbitrary"), vmem_limit_bytes=VMEM_LIMIT),
        name="attn_b",
    )(qbt, kb, vbt)


def _merge_kernel(x_ref, oa_ref, ob_ref, sig_ref, wa_ref, wb_ref, wo_ref, o_ref):
    pa = jnp.dot(oa_ref[0], wa_ref[...], preferred_element_type=F32)
    pb = jnp.dot(ob_ref[0], wb_ref[...], preferred_element_type=F32)
    mix = sig_ref[0, :, :D_MODEL] * pa + sig_ref[0, :, D_MODEL:] * pb
    o_ref[0] = x_ref[0] + jnp.dot(mix.astype(BF16), wo_ref[...], preferred_element_type=F32)


def _merge(x, oa, ob, sig, wa, wb, wo, *, tm):
    b, s, _ = x.shape
    rows = lambda width: pl.BlockSpec((1, tm, width), lambda bi, i: (bi, i, 0))
    wspec = _const_spec((D_MODEL, D_MODEL))
    return pl.pallas_call(
        _merge_kernel,
        grid=(b, s // tm),
        in_specs=[rows(D_MODEL), rows(D_MODEL), rows(D_MODEL), rows(GATE_W), wspec, wspec, wspec],
        out_specs=rows(D_MODEL),
        out_shape=jax.ShapeDtypeStruct((b, s, D_MODEL), F32),
        compiler_params=pltpu.CompilerParams(
            dimension_semantics=("parallel", "parallel"), vmem_limit_bytes=VMEM_LIMIT),
        name="merge",
    )(x, oa, ob, sig, wa, wb, wo)


def _ffn_kernel(xp_ref, x_ref, xn_ref, g_ref, wu_ref, cw_ref, cb_ref, wd_ref, gf_ref, o_ref, h_ref, *, ck):
    tm = x_ref.shape[1]
    i = pl.program_id(1)
    g = g_ref[...]
    x = x_ref[0]
    hp = jnp.where(i > 0, _rms_rows(xp_ref[0], g), 0.0)
    hn = jnp.where(i < pl.num_programs(1) - 1, _rms_rows(xn_ref[0], g), 0.0)
    h_ref[0:HALO] = hp.astype(BF16)
    h_ref[HALO:HALO + tm] = _rms_rows(x, g).astype(BF16)
    h_ref[HALO + tm:] = hn.astype(BF16)
    h = h_ref[...]

    def conv(u, c0):
        w = cw_ref[:, c0:c0 + ck]
        return (u[HALO - 1:HALO - 1 + tm] * w[0:1] + u[HALO:HALO + tm] * w[1:2]
                + u[HALO + 1:HALO + 1 + tm] * w[2:3] + cb_ref[:, c0:c0 + ck])

    acc = jnp.zeros((tm, D_MODEL), F32)
    for c in range(D_FF // ck):
        val = conv(jnp.dot(h, wu_ref[:, c * ck:(c + 1) * ck], preferred_element_type=F32), c * ck)
        gate = conv(jnp.dot(h, wu_ref[:, D_FF + c * ck:D_FF + (c + 1) * ck], preferred_element_type=F32),
                    D_FF + c * ck)
        act = (jax.nn.silu(gate) * val).astype(BF16)
        acc = acc + jnp.dot(act, wd_ref[c * ck:(c + 1) * ck, :], preferred_element_type=F32)
    o_ref[0] = _rms_rows(x + acc, gf_ref[...])


def _ffn(x, g, wu, cw, cb, wd, gf, *, tm, ck):
    b, s, _ = x.shape
    nh = tm // HALO
    last = s // HALO - 1
    return pl.pallas_call(
        functools.partial(_ffn_kernel, ck=ck),
        grid=(b, s // tm),
        in_specs=[
            pl.BlockSpec((1, HALO, D_MODEL), lambda bi, i: (bi, jnp.maximum(i * nh - 1, 0), 0)),
            pl.BlockSpec((1, tm, D_MODEL), lambda bi, i: (bi, i, 0)),
            pl.BlockSpec((1, HALO, D_MODEL), lambda bi, i: (bi, jnp.minimum((i + 1) * nh, last), 0)),
            _const_spec((1, D_MODEL)),
            _const_spec((D_MODEL, 2 * D_FF)),
            _const_spec((3, 2 * D_FF)),
            _const_spec((1, 2 * D_FF)),
            _const_spec((D_FF, D_MODEL)),
            _const_spec((1, D_MODEL)),
        ],
        out_specs=pl.BlockSpec((1, tm, D_MODEL), lambda bi, i: (bi, i, 0)),
        out_shape=jax.ShapeDtypeStruct((b, s, D_MODEL), F32),
        scratch_shapes=[pltpu.VMEM((tm + 2 * HALO, D_MODEL), BF16)],
        compiler_params=pltpu.CompilerParams(
            dimension_semantics=("parallel", "parallel"), vmem_limit_bytes=VMEM_LIMIT),
        name="ffn",
    )(x, x, x, g, wu, cw, cb, wd, gf)


def _rope_angles(pos, dim, theta):
    inv = 1.0 / (theta ** (jnp.arange(0, dim, 2, dtype=F32) / dim))
    ang = pos.astype(F32)[:, None] * inv[None, :]
    return jnp.cos(ang), jnp.sin(ang)


def _rope_tables(s):
    pos = jnp.arange(s)
    ca, sa = _rope_angles(pos, A_ROT_DIM, A_ROPE_THETA)
    ones = jnp.ones((s, A_HEAD_DIM - A_ROT_DIM), F32)
    cos_a = jnp.concatenate([ca, ca, ones], axis=1)
    sin_a = jnp.concatenate([-sa, sa, 0.0 * ones], axis=1)
    half = B_HEAD_DIM // 2
    cr, sr = _rope_angles(pos // GRID_W, half, B_ROPE_THETA)
    cc, sc = _rope_angles(pos % GRID_W, half, B_ROPE_THETA)
    cos_b = jnp.concatenate([cr, cr, cc, cc], axis=1)
    sin_b = jnp.concatenate([-sr, sr, -sc, sc], axis=1)
    two = lambda t: jnp.concatenate([t, t], axis=1)
    return two(cos_a), two(sin_a), two(cos_b), two(sin_b)


def _w_in_columns():
    cols = list(range(C_QB))
    for j in range(B_PAIRS):
        for g in range(B_GROUP):
            for half in range(2):
                head = (2 * j + half) * B_GROUP + g
                cols.extend(range(C_QB + head * B_HEAD_DIM, C_QB + (head + 1) * B_HEAD_DIM))
    cols.extend(range(C_KB, D_IN))
    return jnp.asarray(cols, dtype=jnp.int32)


def _tile(n, pref):
    return pref if n % pref == 0 else n


def _trunk(x, p, tabs):
    s = x.shape[1]
    tk = 512
    nkv = s // tk
    qat, ka, vat, qbt, kb, vbt, sig = _inproj(x, p["norm_mix_g"], p["w_in"], tabs, p["q_norm_g"], p["k_norm_g"],
                                              tm=512, tk=tk)
    unroll = 8 if nkv % 8 == 0 else 2
    oa = _attn_a(p["lam4"], p["subln_g"], qat, ka, vat, tq=2 * A_GROUP_Q, unroll=unroll)
    ob = _attn_b(qbt, kb, vbt, tq=2 * QBLK, unroll=unroll)
    x1 = _merge(x, oa, ob, sig, p["w_proj_a"], p["w_proj_b"], p["w_out"], tm=_tile(s, 512))
    return _ffn(x1, p["norm_ffn_g"], p["w_up"], p["conv_w"], p["conv_b"], p["w_down"], p["norm_final_g"],
                tm=_tile(s, 512), ck=256)


def kernel(x_prompt, x_sample, norm_mix_g, w_in, lam_q1, lam_k1, lam_q2, lam_k2, subln_g, q_norm_g, k_norm_g,
           w_proj_a, w_proj_b, w_out, norm_ffn_g, w_up, conv_w, conv_b, w_down, norm_final_g):
    two = lambda t: jnp.concatenate([t, t]).reshape(1, LANES).astype(F32)
    p = {
        "norm_mix_g": norm_mix_g[0].reshape(1, D_MODEL),
        "w_in": jnp.take(w_in[0], _w_in_columns(), axis=1).astype(BF16),
        "lam4": jnp.stack([lam_q1[0], lam_k1[0], lam_q2[0], lam_k2[0]]).astype(F32),
        "subln_g": subln_g[0].reshape(1, A_V_DIM),
        "q_norm_g": two(q_norm_g[0]),
        "k_norm_g": two(k_norm_g[0]),
        "w_proj_a": w_proj_a[0].astype(BF16),
        "w_proj_b": w_proj_b[0].astype(BF16),
        "w_out": w_out[0].astype(BF16),
        "norm_ffn_g": norm_ffn_g[0].reshape(1, D_MODEL),
        "w_up": w_up[0].astype(BF16),
        "conv_w": conv_w[0],
        "conv_b": conv_b[0].reshape(1, 2 * D_FF),
        "w_down": w_down[0].astype(BF16),
        "norm_final_g": norm_final_g.reshape(1, D_MODEL),
    }
    outs = []
    for x in (x_prompt, x_sample):
        outs.append(_trunk(x, p, _rope_tables(x.shape[1])))
    return tuple(outs)
```

```python
import functools
import math

import jax
import jax.numpy as jnp
from jax import lax
from jax.experimental import pallas as pl
from jax.experimental.pallas import tpu as pltpu

D_MODEL = 1024
GRID_W = 64
EPS = 1e-6
A_HEADS = 8
A_HEAD_DIM = 64
A_V_DIM = 128
A_ROT_DIM = 16
A_ROPE_THETA = 500000.0
B_HEADS = 16
B_KV_HEADS = 4
B_GROUP = 4
B_HEAD_DIM = 64
B_ROPE_THETA = 10000.0
D_FF = 2816
LAM_INIT = 0.8 - 0.6 * math.exp(-0.3 * 0)

A_Q = A_HEADS * 2 * A_HEAD_DIM
A_K = A_Q
A_V = A_HEADS * A_V_DIM
B_Q = B_HEADS * B_HEAD_DIM
B_K = B_KV_HEADS * B_HEAD_DIM
B_V = B_K
GATE_W = 2 * D_MODEL
D_IN = A_Q + A_K + A_V + B_Q + B_K + B_V + GATE_W
C_QA = 0
C_KA = C_QA + A_Q
C_VA = C_KA + A_K
C_QB = C_VA + A_V
C_KB = C_QB + B_Q
C_VB = C_KB + B_K
C_GATE = C_VB + B_V

LANES = 128
HALF = 64
B_PAIRS = B_KV_HEADS // 2
HALO = 16
QBLK = 256
A_GROUP_Q = 2 * QBLK
ONES_ROWS = 16
VA_ROWS = A_V_DIM + ONES_ROWS
VB_ROWS = 2 * (B_HEAD_DIM + ONES_ROWS)
VMEM_LIMIT = 56 * 1024 * 1024

F32 = jnp.float32
BF16 = jnp.bfloat16


def _rms_rows(x, g):
    return x * lax.rsqrt(jnp.mean(x * x, axis=-1, keepdims=True) + EPS) * g


def _const_spec(shape):
    zeros = (0,) * len(shape)
    return pl.BlockSpec(shape, lambda *_: zeros, pipeline_mode=pl.Buffered(1))


def _inproj_kernel(x_ref, g_ref, w_ref, ca_ref, sa_ref, cb_ref, sb_ref, qg_ref, kg_ref,
                   qat_ref, ka_ref, vat_ref, qbt_ref, kb_ref, vbt_ref, sig_ref):
    tm = x_ref.shape[1]
    h = _rms_rows(x_ref[0], g_ref[...]).astype(BF16)
    lane = lax.broadcasted_iota(jnp.int32, (tm, LANES), 1)
    lo_half = lane < HALF
    first8 = (lane & 8) == 0
    first16 = (lane & 16) == 0
    ca, sa, cb, sb = ca_ref[...], sa_ref[...], cb_ref[...], sb_ref[...]
    qg, kg = qg_ref[...], kg_ref[...]
    scale = A_HEAD_DIM ** -0.5 * math.log2(math.e)

    def proj(c0, n):
        return jnp.dot(h, w_ref[:, c0:c0 + n], preferred_element_type=F32)

    def rope_a(t):
        partner = jnp.where(first8, pltpu.roll(t, LANES - 8, 1), pltpu.roll(t, 8, 1))
        return t * ca + partner * sa

    def rope_b(t):
        partner = jnp.where(first16, pltpu.roll(t, LANES - 16, 1), pltpu.roll(t, 16, 1))
        return t * cb + partner * sb

    def headnorm(t, g):
        sq = t * t
        s_lo = jnp.sum(jnp.where(lo_half, sq, 0.0), axis=1, keepdims=True)
        s_hi = jnp.sum(jnp.where(lo_half, 0.0, sq), axis=1, keepdims=True)
        ms = jnp.where(lo_half, s_lo, s_hi) * (1.0 / HALF)
        return t * lax.rsqrt(ms + EPS) * g

    ones_rows = jnp.ones((ONES_ROWS, tm), BF16)

    def store_vt(ref, idx, t, dv):
        tt = t.T.astype(BF16)
        for grp in range(LANES // dv):
            r0 = grp * (dv + ONES_ROWS)
            ref[idx + (0, slice(r0, r0 + dv))] = tt[grp * dv:(grp + 1) * dv]
            ref[idx + (0, slice(r0 + dv, r0 + dv + ONES_ROWS))] = ones_rows

    def blocks(acc):
        return [acc[:, i * LANES:(i + 1) * LANES] for i in range(acc.shape[1] // LANES)]

    def qa_out(p, acc):
        for i, t in enumerate(blocks(acc)):
            qat_ref[0, 2 * p + i] = (rope_a(t) * scale).T.astype(BF16)

    def ka_out(p, acc):
        for i, t in enumerate(blocks(acc)):
            ka_ref[0, 2 * p + i] = rope_a(t).astype(BF16)

    def va_out(p, acc):
        for i, t in enumerate(blocks(acc)):
            store_vt(vat_ref, (0, 2 * p + i), t, A_V_DIM)

    def qb_out(p, acc):
        for i, t in enumerate(blocks(acc)):
            blk = 2 * p + i
            qbt_ref[0, blk // B_GROUP, blk % B_GROUP] = (rope_b(headnorm(t, qg)) * scale).T.astype(BF16)

    def kb_out(p, acc):
        for j, t in enumerate(blocks(acc)):
            kb_ref[0, j] = rope_b(headnorm(t, kg)).astype(BF16)

    def vb_out(p, acc):
        for j, t in enumerate(blocks(acc)):
            store_vt(vbt_ref, (0, j), t, B_HEAD_DIM)

    def gate_out(p, acc):
        sig_ref[0, :, 2 * LANES * p:2 * LANES * (p + 1)] = jax.nn.sigmoid(acc).astype(sig_ref.dtype)

    stages = ([(C_QA, qa_out, p) for p in range(A_HEADS // 2)] + [(C_KA, ka_out, p) for p in range(A_HEADS // 2)]
              + [(C_VA, va_out, p) for p in range(A_HEADS // 2)]
              + [(C_QB, qb_out, p) for p in range(B_PAIRS * B_GROUP // 2)] + [(C_KB, kb_out, 0), (C_VB, vb_out, 0)]
              + [(C_GATE, gate_out, p) for p in range(GATE_W // (2 * LANES))])
    for c0, out, p in stages:
        out(p, proj(c0 + 2 * LANES * p, 2 * LANES))


def _inproj(x, g, w, tabs, qg, kg, *, tm, tk):
    b, s, _ = x.shape
    nt = s // tk
    sub = tk // tm
    grid = (b, s // tm)
    row = lambda bi, i: (i, 0)
    out_shapes = (
        jax.ShapeDtypeStruct((b, A_HEADS, LANES, s), BF16),
        jax.ShapeDtypeStruct((b, A_HEADS, s, LANES), BF16),
        jax.ShapeDtypeStruct((b, A_HEADS, nt, VA_ROWS, tk), BF16),
        jax.ShapeDtypeStruct((b, B_PAIRS, B_GROUP, LANES, s), BF16),
        jax.ShapeDtypeStruct((b, B_PAIRS, s, LANES), BF16),
        jax.ShapeDtypeStruct((b, B_PAIRS, nt, VB_ROWS, tk), BF16),
        jax.ShapeDtypeStruct((b, s, GATE_W), BF16),
    )
    out_specs = (
        pl.BlockSpec((1, A_HEADS, LANES, tm), lambda bi, i: (bi, 0, 0, i)),
        pl.BlockSpec((1, A_HEADS, tm, LANES), lambda bi, i: (bi, 0, i, 0)),
        pl.BlockSpec((1, A_HEADS, 1, VA_ROWS, tm), lambda bi, i: (bi, 0, i // sub, 0, i % sub)),
        pl.BlockSpec((1, B_PAIRS, B_GROUP, LANES, tm), lambda bi, i: (bi, 0, 0, 0, i)),
        pl.BlockSpec((1, B_PAIRS, tm, LANES), lambda bi, i: (bi, 0, i, 0)),
        pl.BlockSpec((1, B_PAIRS, 1, VB_ROWS, tm), lambda bi, i: (bi, 0, i // sub, 0, i % sub)),
        pl.BlockSpec((1, tm, GATE_W), lambda bi, i: (bi, i, 0)),
    )
    in_specs = [
        pl.BlockSpec((1, tm, D_MODEL), lambda bi, i: (bi, i, 0)),
        _const_spec((1, D_MODEL)),
        _const_spec((D_MODEL, D_IN)),
        pl.BlockSpec((tm, LANES), row), pl.BlockSpec((tm, LANES), row),
        pl.BlockSpec((tm, LANES), row), pl.BlockSpec((tm, LANES), row),
        _const_spec((1, LANES)), _const_spec((1, LANES)),
    ]
    return pl.pallas_call(
        _inproj_kernel, grid=grid, in_specs=in_specs, out_specs=out_specs, out_shape=out_shapes,
        compiler_params=pltpu.CompilerParams(
            dimension_semantics=("parallel", "parallel"), vmem_limit_bytes=VMEM_LIMIT),
        name="inproj",
    )(x, g, w, *tabs, qg, kg)


def _flash_attention(k_tile, vt_tile, qs, s_ref, acc_ref, nkv, unroll):
    n = len(qs)
    assert unroll % 2 == 0 and nkv % unroll == 0

    def score(k, c, slot):
        s = jnp.dot(k, qs[c], preferred_element_type=F32)
        s_ref[c, slot] = s
        return jnp.max(s, axis=0, keepdims=True)

    def scores(j, slot):
        k = k_tile(j)
        return tuple(score(k, c, slot) for c in range(n))

    def update(j, c, slot, mx, m):
        m_new = jnp.maximum(m, mx)
        alpha = jnp.exp2(m - m_new)
        p = jnp.exp2(s_ref[c, slot] - m_new).astype(BF16)
        acc_ref[c] = alpha * acc_ref[c] + jnp.dot(vt_tile(j, c), p, preferred_element_type=F32)
        return m_new

    acc_ref[...] = jnp.zeros(acc_ref.shape, F32)
    tq = qs[0].shape[1]
    m0 = (jnp.full((1, tq), -jnp.inf, F32),) * n

    def tiles(j0, m, mx, last):
        m = list(m)
        for t in range(unroll):
            skip = last and t == unroll - 1
            k = None if skip else k_tile(j0 + t + 1)
            mx_next = []
            for c in range(n):
                if not skip:
                    mx_next.append(score(k, c, (t + 1) % 2))
                m[c] = update(j0 + t, c, t % 2, mx[c], m[c])
            mx = tuple(mx_next)
        return tuple(m), mx

    carry = (m0, scores(0, 0))
    carry = lax.fori_loop(0, nkv // unroll - 1, lambda i, c: tiles(i * unroll, *c, False), carry)
    tiles(nkv - unroll, *carry, True)


def _key_rows(j, tk):
    start = j * tk
    return pl.ds(start if isinstance(start, int) else pl.multiple_of(start, tk), tk)


def _attn_a_kernel(lam_ref, sg_ref, qt_ref, k_ref, vt_ref, o_ref, s_ref, acc_ref, *, unroll):
    tq = qt_ref.shape[3]
    nkv, tk = vt_ref.shape[2], vt_ref.shape[4]
    row = lax.broadcasted_iota(jnp.int32, (LANES, QBLK), 0)
    lam_v = lam_ref[...]
    lam = (jnp.exp(jnp.sum(lam_v[0:1] * lam_v[1:2], axis=1, keepdims=True))
           - jnp.exp(jnp.sum(lam_v[2:3] * lam_v[3:4], axis=1, keepdims=True)) + LAM_INIT)
    for grp in range(tq // A_GROUP_Q):
        acc = acc_ref.at[grp]
        blocks = [grp * A_GROUP_Q + i * QBLK for i in range(A_GROUP_Q // QBLK)]
        qs = []
        for q0 in blocks:
            qt = qt_ref[0, 0, :, q0:q0 + QBLK]
            zero = jnp.zeros_like(qt)
            qs += [jnp.where(row < HALF, qt, zero), jnp.where(row >= HALF, qt, zero)]
        _flash_attention(
            lambda j: k_ref[0, 0, _key_rows(j, tk), :],
            lambda j, c: vt_ref[0, 0, j],
            qs, s_ref, acc, nkv, unroll)
        for i, q0 in enumerate(blocks):
            o1 = acc[2 * i, :A_V_DIM] / acc[2 * i, A_V_DIM:A_V_DIM + 1]
            o2 = acc[2 * i + 1, :A_V_DIM] / acc[2 * i + 1, A_V_DIM:A_V_DIM + 1]
            o = o1 - lam * o2
            y = _rms_rows(o.T, sg_ref[...]) * (1.0 - LAM_INIT)
            o_ref[0, q0:q0 + QBLK, :] = y.astype(BF16)


def _attn_a(lam4, sg, qat, ka, vat, *, tq, unroll):
    b, _, _, s = qat.shape
    nkv, tk = vat.shape[2], vat.shape[4]
    chains = 2 * A_GROUP_Q // QBLK
    return pl.pallas_call(
        functools.partial(_attn_a_kernel, unroll=unroll),
        grid=(b, A_HEADS, s // tq),
        in_specs=[
            _const_spec((4, A_HEAD_DIM)), _const_spec((1, A_V_DIM)),
            pl.BlockSpec((1, 1, LANES, tq), lambda bi, h, i: (bi, h, 0, i)),
            pl.BlockSpec((1, 1, s, LANES), lambda bi, h, i: (bi, h, 0, 0)),
            pl.BlockSpec((1, 1, nkv, VA_ROWS, tk), lambda bi, h, i: (bi, h, 0, 0, 0)),
        ],
        out_specs=pl.BlockSpec((1, tq, A_V_DIM), lambda bi, h, i: (bi, i, h)),
        out_shape=jax.ShapeDtypeStruct((b, s, A_HEADS * A_V_DIM), BF16),
        scratch_shapes=[pltpu.VMEM((chains, 2, tk, QBLK), F32),
                        pltpu.VMEM((tq // A_GROUP_Q, chains, VA_ROWS, QBLK), F32)],
        compiler_params=pltpu.CompilerParams(
            dimension_semantics=("parallel", "parallel", "arbitrary"), vmem_limit_bytes=VMEM_LIMIT),
        name="attn_a",
    )(lam4, sg, qat, ka, vat)


def _attn_b_kernel(qt_ref, k_ref, vt_ref, o_ref, s_ref, acc_ref, *, unroll):
    tq = qt_ref.shape[4]
    nkv, tk = vt_ref.shape[2], vt_ref.shape[4]
    dv = B_HEAD_DIM
    row = lax.broadcasted_iota(jnp.int32, (LANES, QBLK), 0)
    first = pl.program_id(2) * HALF
    keep = (row >= first) & (row < first + HALF)
    for grp in range(tq // QBLK):
        acc = acc_ref.at[grp]
        q0 = grp * QBLK
        qs = []
        for g in range(B_GROUP):
            qt = qt_ref[0, 0, g, :, q0:q0 + QBLK]
            qs.append(jnp.where(keep, qt, jnp.zeros_like(qt)))
        _flash_attention(lambda j: k_ref[0, 0, _key_rows(j, tk), :], lambda j, c: vt_ref[0, 0, j],
                         qs, s_ref, acc, nkv, unroll)
        for blk in range(B_GROUP // 2):
            parts = [acc[c, :dv] / acc[c, dv:dv + 1] for c in (2 * blk, 2 * blk + 1)]
            o_ref[0, q0:q0 + QBLK, blk * LANES:(blk + 1) * LANES] = (
                jnp.concatenate(parts, axis=0).T.astype(BF16))


def _attn_b(qbt, kb, vbt, *, tq, unroll):
    b, _, _, _, s = qbt.shape
    nkv, tk = vbt.shape[2], vbt.shape[4]
    width = B_GROUP * B_HEAD_DIM
    rows = B_HEAD_DIM + ONES_ROWS
    return pl.pallas_call(
        functools.partial(_attn_b_kernel, unroll=unroll),
        grid=(b, B_PAIRS, 2, s // tq),
        in_specs=[
            pl.BlockSpec((1, 1, B_GROUP, LANES, tq), lambda bi, j, hf, i: (bi, j, 0, 0, i)),
            pl.BlockSpec((1, 1, s, LANES), lambda bi, j, hf, i: (bi, j, 0, 0)),
            pl.BlockSpec((1, 1, nkv, rows, tk), lambda bi, j, hf, i: (bi, j, 0, hf, 0)),
        ],
        out_specs=pl.BlockSpec((1, tq, width), lambda bi, j, hf, i: (bi, i, 2 * j + hf)),
        out_shape=jax.ShapeDtypeStruct((b, s, B_HEADS * B_HEAD_DIM), BF16),
        scratch_shapes=[pltpu.VMEM((B_GROUP, 2, tk, QBLK), F32),
                        pltpu.VMEM((tq // QBLK, B_GROUP, rows, QBLK), F32)],
        compiler_params=pltpu.CompilerParams(
            dimension_semantics=("parallel", "parallel", "parallel", "arbitrary"), vmem_limit_bytes=VMEM_LIMIT),
        name="attn_b",
    )(qbt, kb, vbt)


def _merge_kernel(x_ref, oa_ref, ob_ref, sig_ref, wa_ref, wb_ref, wo_ref, o_ref):
    pa = jnp.dot(oa_ref[0], wa_ref[...], preferred_element_type=F32)
    pb = jnp.dot(ob_ref[0], wb_ref[...], preferred_element_type=F32)
    mix = sig_ref[0, :, :D_MODEL] * pa + sig_ref[0, :, D_MODEL:] * pb
    o_ref[0] = x_ref[0] + jnp.dot(mix.astype(BF16), wo_ref[...], preferred_element_type=F32)


def _merge(x, oa, ob, sig, wa, wb, wo, *, tm):
    b, s, _ = x.shape
    rows = lambda width: pl.BlockSpec((1, tm, width), lambda bi, i: (bi, i, 0))
    wspec = _const_spec((D_MODEL, D_MODEL))
    return pl.pallas_call(
        _merge_kernel,
        grid=(b, s // tm),
        in_specs=[rows(D_MODEL), rows(D_MODEL), rows(D_MODEL), rows(GATE_W), wspec, wspec, wspec],
        out_specs=rows(D_MODEL),
        out_shape=jax.ShapeDtypeStruct((b, s, D_MODEL), F32),
        compiler_params=pltpu.CompilerParams(
            dimension_semantics=("parallel", "parallel"), vmem_limit_bytes=VMEM_LIMIT),
        name="merge",
    )(x, oa, ob, sig, wa, wb, wo)


def _ffn_kernel(xp_ref, x_ref, xn_ref, g_ref, wu_ref, cw_ref, cb_ref, wd_ref, gf_ref, o_ref, h_ref, *, ck):
    tm = x_ref.shape[1]
    i = pl.program_id(1)
    g = g_ref[...]
    x = x_ref[0]
    hp = jnp.where(i > 0, _rms_rows(xp_ref[0], g), 0.0)
    hn = jnp.where(i < pl.num_programs(1) - 1, _rms_rows(xn_ref[0], g), 0.0)
    h_ref[0:HALO] = hp.astype(BF16)
    h_ref[HALO:HALO + tm] = _rms_rows(x, g).astype(BF16)
    h_ref[HALO + tm:] = hn.astype(BF16)
    h = h_ref[...]

    def conv(u, c0):
        w = cw_ref[:, c0:c0 + ck]
        rows = u.shape[0]
        prev = pltpu.roll(u, 1, 0)[HALO:HALO + tm]
        nxt = pltpu.roll(u, rows - 1, 0)[HALO:HALO + tm]
        return prev * w[0:1] + u[HALO:HALO + tm] * w[1:2] + nxt * w[2:3] + cb_ref[:, c0:c0 + ck]

    def up(c):
        return (jnp.dot(h, wu_ref[:, c * ck:(c + 1) * ck], preferred_element_type=F32),
                jnp.dot(h, wu_ref[:, D_FF + c * ck:D_FF + (c + 1) * ck], preferred_element_type=F32))

    nc = D_FF // ck
    acc = jnp.zeros((tm, D_MODEL), F32)
    u_next = up(0)
    for c in range(nc):
        uv, ug = u_next
        if c + 1 < nc:
            u_next = up(c + 1)
        act = (jax.nn.silu(conv(ug, D_FF + c * ck)) * conv(uv, c * ck)).astype(BF16)
        acc = acc + jnp.dot(act, wd_ref[c * ck:(c + 1) * ck, :], preferred_element_type=F32)
    o_ref[0] = _rms_rows(x + acc, gf_ref[...])


def _ffn(x, g, wu, cw, cb, wd, gf, *, tm, ck):
    b, s, _ = x.shape
    nh = tm // HALO
    last = s // HALO - 1
    return pl.pallas_call(
        functools.partial(_ffn_kernel, ck=ck),
        grid=(b, s // tm),
        in_specs=[
            pl.BlockSpec((1, HALO, D_MODEL), lambda bi, i: (bi, jnp.maximum(i * nh - 1, 0), 0)),
            pl.BlockSpec((1, tm, D_MODEL), lambda bi, i: (bi, i, 0)),
            pl.BlockSpec((1, HALO, D_MODEL), lambda bi, i: (bi, jnp.minimum((i + 1) * nh, last), 0)),
            _const_spec((1, D_MODEL)),
            _const_spec((D_MODEL, 2 * D_FF)),
            _const_spec((3, 2 * D_FF)),
            _const_spec((1, 2 * D_FF)),
            _const_spec((D_FF, D_MODEL)),
            _const_spec((1, D_MODEL)),
        ],
        out_specs=pl.BlockSpec((1, tm, D_MODEL), lambda bi, i: (bi, i, 0)),
        out_shape=jax.ShapeDtypeStruct((b, s, D_MODEL), F32),
        scratch_shapes=[pltpu.VMEM((tm + 2 * HALO, D_MODEL), BF16)],
        compiler_params=pltpu.CompilerParams(
            dimension_semantics=("parallel", "parallel"), vmem_limit_bytes=VMEM_LIMIT),
        name="ffn",
    )(x, x, x, g, wu, cw, cb, wd, gf)


def _rope_angles(pos, dim, theta):
    inv = 1.0 / (theta ** (jnp.arange(0, dim, 2, dtype=F32) / dim))
    ang = pos.astype(F32)[:, None] * inv[None, :]
    return jnp.cos(ang), jnp.sin(ang)


def _rope_tables(s):
    pos = jnp.arange(s)
    ca, sa = _rope_angles(pos, A_ROT_DIM, A_ROPE_THETA)
    ones = jnp.ones((s, A_HEAD_DIM - A_ROT_DIM), F32)
    cos_a = jnp.concatenate([ca, ca, ones], axis=1)
    sin_a = jnp.concatenate([-sa, sa, 0.0 * ones], axis=1)
    half = B_HEAD_DIM // 2
    cr, sr = _rope_angles(pos // GRID_W, half, B_ROPE_THETA)
    cc, sc = _rope_angles(pos % GRID_W, half, B_ROPE_THETA)
    cos_b = jnp.concatenate([cr, cr, cc, cc], axis=1)
    sin_b = jnp.concatenate([-sr, sr, -sc, sc], axis=1)
    two = lambda t: jnp.concatenate([t, t], axis=1)
    return two(cos_a), two(sin_a), two(cos_b), two(sin_b)


def _w_in_columns():
    cols = list(range(C_QB))
    for j in range(B_PAIRS):
        for g in range(B_GROUP):
            for half in range(2):
                head = (2 * j + half) * B_GROUP + g
                cols.extend(range(C_QB + head * B_HEAD_DIM, C_QB + (head + 1) * B_HEAD_DIM))
    cols.extend(range(C_KB, D_IN))
    return jnp.asarray(cols, dtype=jnp.int32)


def _tile(n, pref):
    return pref if n % pref == 0 else n


def _trunk(x, p, tabs):
    s = x.shape[1]
    tk = 512
    nkv = s // tk
    qat, ka, vat, qbt, kb, vbt, sig = _inproj(x, p["norm_mix_g"], p["w_in"], tabs, p["q_norm_g"], p["k_norm_g"],
                                              tm=512, tk=tk)
    unroll = 8 if nkv % 8 == 0 else 2
    oa = _attn_a(p["lam4"], p["subln_g"], qat, ka, vat, tq=2 * A_GROUP_Q, unroll=unroll)
    ob = _attn_b(qbt, kb, vbt, tq=2 * QBLK, unroll=unroll)
    x1 = _merge(x, oa, ob, sig, p["w_proj_a"], p["w_proj_b"], p["w_out"], tm=_tile(s, 512))
    return _ffn(x1, p["norm_ffn_g"], p["w_up"], p["conv_w"], p["conv_b"], p["w_down"], p["norm_final_g"],
                tm=_tile(s, 1024), ck=256)


def kernel(x_prompt, x_sample, norm_mix_g, w_in, lam_q1, lam_k1, lam_q2, lam_k2, subln_g, q_norm_g, k_norm_g,
           w_proj_a, w_proj_b, w_out, norm_ffn_g, w_up, conv_w, conv_b, w_down, norm_final_g):
    two = lambda t: jnp.concatenate([t, t]).reshape(1, LANES).astype(F32)
    p = {
        "norm_mix_g": norm_mix_g[0].reshape(1, D_MODEL),
        "w_in": jnp.take(w_in[0], _w_in_columns(), axis=1).astype(BF16),
        "lam4": jnp.stack([lam_q1[0], lam_k1[0], lam_q2[0], lam_k2[0]]).astype(F32),
        "subln_g": subln_g[0].reshape(1, A_V_DIM),
        "q_norm_g": two(q_norm_g[0]),
        "k_norm_g": two(k_norm_g[0]),
        "w_proj_a": w_proj_a[0].astype(BF16),
        "w_proj_b": w_proj_b[0].astype(BF16),
        "w_out": w_out[0].astype(BF16),
        "norm_ffn_g": norm_ffn_g[0].reshape(1, D_MODEL),
        "w_up": w_up[0].astype(BF16),
        "conv_w": conv_w[0],
        "conv_b": conv_b[0].reshape(1, 2 * D_FF),
        "w_down": w_down[0].astype(BF16),
        "norm_final_g": norm_final_g.reshape(1, D_MODEL),
    }
    tables = {}
    outs = []
    for x in (x_prompt, x_sample):
        s = x.shape[1]
        if s not in tables:
            tables[s] = _rope_tables(s)
        outs.append(_trunk(x, p, tables[s]))
    return tuple(outs)
```

```python
import functools
import math

import jax
import jax.numpy as jnp
from jax import lax
from jax.experimental import pallas as pl
from jax.experimental.pallas import tpu as pltpu

D_MODEL = 1024
GRID_W = 64
EPS = 1e-6
A_HEADS = 8
A_HEAD_DIM = 64
A_V_DIM = 128
A_ROT_DIM = 16
A_ROPE_THETA = 500000.0
B_HEADS = 16
B_KV_HEADS = 4
B_GROUP = 4
B_HEAD_DIM = 64
B_ROPE_THETA = 10000.0
D_FF = 2816
LAM_INIT = 0.8 - 0.6 * math.exp(-0.3 * 0)

A_Q = A_HEADS * 2 * A_HEAD_DIM
A_K = A_Q
A_V = A_HEADS * A_V_DIM
B_Q = B_HEADS * B_HEAD_DIM
B_K = B_KV_HEADS * B_HEAD_DIM
B_V = B_K
GATE_W = 2 * D_MODEL
D_IN = A_Q + A_K + A_V + B_Q + B_K + B_V + GATE_W
C_QA = 0
C_KA = C_QA + A_Q
C_VA = C_KA + A_K
C_QB = C_VA + A_V
C_KB = C_QB + B_Q
C_VB = C_KB + B_K
C_GATE = C_VB + B_V

LANES = 128
HALF = 64
B_PAIRS = B_KV_HEADS // 2
HALO = 16
QBLK = 256
A_GROUP_Q = 2 * QBLK
ONES_ROWS = 16
VA_ROWS = A_V_DIM + ONES_ROWS
VB_ROWS = 2 * (B_HEAD_DIM + ONES_ROWS)
VMEM_LIMIT = 56 * 1024 * 1024

F32 = jnp.float32
BF16 = jnp.bfloat16


def _rms_rows(x, g):
    return x * lax.rsqrt(jnp.mean(x * x, axis=-1, keepdims=True) + EPS) * g


def _const_spec(shape):
    zeros = (0,) * len(shape)
    return pl.BlockSpec(shape, lambda *_: zeros, pipeline_mode=pl.Buffered(1))


def _inproj_kernel(x_ref, g_ref, w_ref, ca_ref, sa_ref, cb_ref, sb_ref, qg_ref, kg_ref,
                   qat_ref, ka_ref, vat_ref, qbt_ref, kb_ref, vbt_ref, sig_ref):
    tm = x_ref.shape[1]
    h = _rms_rows(x_ref[0], g_ref[...]).astype(BF16)
    lane = lax.broadcasted_iota(jnp.int32, (tm, LANES), 1)
    lo_half = lane < HALF
    first8 = (lane & 8) == 0
    first16 = (lane & 16) == 0
    ca, sa, cb, sb = ca_ref[...], sa_ref[...], cb_ref[...], sb_ref[...]
    qg, kg = qg_ref[...], kg_ref[...]
    scale = A_HEAD_DIM ** -0.5 * math.log2(math.e)

    def proj(c0, n):
        return jnp.dot(h, w_ref[:, c0:c0 + n], preferred_element_type=F32)

    def rope_a(t):
        partner = jnp.where(first8, pltpu.roll(t, LANES - 8, 1), pltpu.roll(t, 8, 1))
        return t * ca + partner * sa

    def rope_b(t):
        partner = jnp.where(first16, pltpu.roll(t, LANES - 16, 1), pltpu.roll(t, 16, 1))
        return t * cb + partner * sb

    def headnorm(t, g):
        sq = t * t
        s_lo = jnp.sum(jnp.where(lo_half, sq, 0.0), axis=1, keepdims=True)
        s_hi = jnp.sum(jnp.where(lo_half, 0.0, sq), axis=1, keepdims=True)
        ms = jnp.where(lo_half, s_lo, s_hi) * (1.0 / HALF)
        return t * lax.rsqrt(ms + EPS) * g

    ones_rows = jnp.ones((ONES_ROWS, tm), BF16)

    def store_vt(ref, idx, t, dv):
        tt = t.T.astype(BF16)
        for grp in range(LANES // dv):
            r0 = grp * (dv + ONES_ROWS)
            ref[idx + (0, slice(r0, r0 + dv))] = tt[grp * dv:(grp + 1) * dv]
            ref[idx + (0, slice(r0 + dv, r0 + dv + ONES_ROWS))] = ones_rows

    def blocks(acc):
        return [acc[:, i * LANES:(i + 1) * LANES] for i in range(acc.shape[1] // LANES)]

    def qa_out(p, acc):
        for i, t in enumerate(blocks(acc)):
            qat_ref[0, 2 * p + i] = (rope_a(t) * scale).T.astype(BF16)

    def ka_out(p, acc):
        for i, t in enumerate(blocks(acc)):
            ka_ref[0, 2 * p + i] = rope_a(t).astype(BF16)

    def va_out(p, acc):
        for i, t in enumerate(blocks(acc)):
            store_vt(vat_ref, (0, 2 * p + i), t, A_V_DIM)

    def qb_out(p, acc):
        for i, t in enumerate(blocks(acc)):
            blk = 2 * p + i
            qbt_ref[0, blk // B_GROUP, blk % B_GROUP] = (rope_b(headnorm(t, qg)) * scale).T.astype(BF16)

    def kb_out(p, acc):
        for j, t in enumerate(blocks(acc)):
            kb_ref[0, j] = rope_b(headnorm(t, kg)).astype(BF16)

    def vb_out(p, acc):
        for j, t in enumerate(blocks(acc)):
            store_vt(vbt_ref, (0, j), t, B_HEAD_DIM)

    def gate_out(p, acc):
        sig_ref[0, :, 2 * LANES * p:2 * LANES * (p + 1)] = jax.nn.sigmoid(acc).astype(sig_ref.dtype)

    stages = ([(C_QA, qa_out, p) for p in range(A_HEADS // 2)] + [(C_KA, ka_out, p) for p in range(A_HEADS // 2)]
              + [(C_VA, va_out, p) for p in range(A_HEADS // 2)]
              + [(C_QB, qb_out, p) for p in range(B_PAIRS * B_GROUP // 2)] + [(C_KB, kb_out, 0), (C_VB, vb_out, 0)]
              + [(C_GATE, gate_out, p) for p in range(GATE_W // (2 * LANES))])
    for c0, out, p in stages:
        out(p, proj(c0 + 2 * LANES * p, 2 * LANES))


def _inproj(x, g, w, tabs, qg, kg, *, tm, tk):
    b, s, _ = x.shape
    nt = s // tk
    sub = tk // tm
    grid = (b, s // tm)
    row = lambda bi, i: (i, 0)
    out_shapes = (
        jax.ShapeDtypeStruct((b, A_HEADS, LANES, s), BF16),
        jax.ShapeDtypeStruct((b, A_HEADS, s, LANES), BF16),
        jax.ShapeDtypeStruct((b, A_HEADS, nt, VA_ROWS, tk), BF16),
        jax.ShapeDtypeStruct((b, B_PAIRS, B_GROUP, LANES, s), BF16),
        jax.ShapeDtypeStruct((b, B_PAIRS, s, LANES), BF16),
        jax.ShapeDtypeStruct((b, B_PAIRS, nt, VB_ROWS, tk), BF16),
        jax.ShapeDtypeStruct((b, s, GATE_W), BF16),
    )
    out_specs = (
        pl.BlockSpec((1, A_HEADS, LANES, tm), lambda bi, i: (bi, 0, 0, i)),
        pl.BlockSpec((1, A_HEADS, tm, LANES), lambda bi, i: (bi, 0, i, 0)),
        pl.BlockSpec((1, A_HEADS, 1, VA_ROWS, tm), lambda bi, i: (bi, 0, i // sub, 0, i % sub)),
        pl.BlockSpec((1, B_PAIRS, B_GROUP, LANES, tm), lambda bi, i: (bi, 0, 0, 0, i)),
        pl.BlockSpec((1, B_PAIRS, tm, LANES), lambda bi, i: (bi, 0, i, 0)),
        pl.BlockSpec((1, B_PAIRS, 1, VB_ROWS, tm), lambda bi, i: (bi, 0, i // sub, 0, i % sub)),
        pl.BlockSpec((1, tm, GATE_W), lambda bi, i: (bi, i, 0)),
    )
    in_specs = [
        pl.BlockSpec((1, tm, D_MODEL), lambda bi, i: (bi, i, 0)),
        _const_spec((1, D_MODEL)),
        _const_spec((D_MODEL, D_IN)),
        pl.BlockSpec((tm, LANES), row), pl.BlockSpec((tm, LANES), row),
        pl.BlockSpec((tm, LANES), row), pl.BlockSpec((tm, LANES), row),
        _const_spec((1, LANES)), _const_spec((1, LANES)),
    ]
    return pl.pallas_call(
        _inproj_kernel, grid=grid, in_specs=in_specs, out_specs=out_specs, out_shape=out_shapes,
        compiler_params=pltpu.CompilerParams(
            dimension_semantics=("parallel", "parallel"), vmem_limit_bytes=VMEM_LIMIT),
        name="inproj",
    )(x, g, w, *tabs, qg, kg)


def _flash_attention(k_tile, vt_tile, qs, s_ref, acc_ref, nkv, unroll):
    n = len(qs)
    assert unroll % 2 == 0 and nkv % unroll == 0

    def score(k, c, slot):
        s = jnp.dot(k, qs[c], preferred_element_type=F32)
        s_ref[c, slot] = s
        return jnp.max(s, axis=0, keepdims=True)

    def scores(j, slot):
        k = k_tile(j)
        return tuple(score(k, c, slot) for c in range(n))

    def update(j, c, slot, mx, m):
        m_new = jnp.maximum(m, mx)
        alpha = jnp.exp2(m - m_new)
        p = jnp.exp2(s_ref[c, slot] - m_new).astype(BF16)
        acc_ref[c] = alpha * acc_ref[c] + jnp.dot(vt_tile(j, c), p, preferred_element_type=F32)
        return m_new

    acc_ref[...] = jnp.zeros(acc_ref.shape, F32)
    tq = qs[0].shape[1]
    m0 = (jnp.full((1, tq), -jnp.inf, F32),) * n

    def tiles(j0, m, mx, last):
        m = list(m)
        for t in range(unroll):
            skip = last and t == unroll - 1
            k = None if skip else k_tile(j0 + t + 1)
            mx_next = []
            for c in range(n):
                if not skip:
                    mx_next.append(score(k, c, (t + 1) % 2))
                m[c] = update(j0 + t, c, t % 2, mx[c], m[c])
            mx = tuple(mx_next)
        return tuple(m), mx

    carry = (m0, scores(0, 0))
    carry = lax.fori_loop(0, nkv // unroll - 1, lambda i, c: tiles(i * unroll, *c, False), carry)
    tiles(nkv - unroll, *carry, True)


def _key_rows(j, tk):
    start = j * tk
    return pl.ds(start if isinstance(start, int) else pl.multiple_of(start, tk), tk)


def _attn_a_kernel(lam_ref, sg_ref, qt_ref, k_ref, vt_ref, o_ref, s_ref, acc_ref, *, unroll):
    tq = qt_ref.shape[3]
    nkv, tk = vt_ref.shape[2], vt_ref.shape[4]
    row = lax.broadcasted_iota(jnp.int32, (LANES, QBLK), 0)
    lam_v = lam_ref[...]
    lam = (jnp.exp(jnp.sum(lam_v[0:1] * lam_v[1:2], axis=1, keepdims=True))
           - jnp.exp(jnp.sum(lam_v[2:3] * lam_v[3:4], axis=1, keepdims=True)) + LAM_INIT)
    for grp in range(tq // A_GROUP_Q):
        acc = acc_ref.at[grp]
        blocks = [grp * A_GROUP_Q + i * QBLK for i in range(A_GROUP_Q // QBLK)]
        qs = []
        for q0 in blocks:
            qt = qt_ref[0, 0, :, q0:q0 + QBLK]
            zero = jnp.zeros_like(qt)
            qs += [jnp.where(row < HALF, qt, zero), jnp.where(row >= HALF, qt, zero)]
        _flash_attention(
            lambda j: k_ref[0, 0, _key_rows(j, tk), :],
            lambda j, c: vt_ref[0, 0, j],
            qs, s_ref, acc, nkv, unroll)
        for i, q0 in enumerate(blocks):
            o1 = acc[2 * i, :A_V_DIM] / acc[2 * i, A_V_DIM:A_V_DIM + 1]
            o2 = acc[2 * i + 1, :A_V_DIM] / acc[2 * i + 1, A_V_DIM:A_V_DIM + 1]
            o = o1 - lam * o2
            y = _rms_rows(o.T, sg_ref[...]) * (1.0 - LAM_INIT)
            o_ref[0, q0:q0 + QBLK, :] = y.astype(BF16)


def _attn_a(lam4, sg, qat, ka, vat, *, tq, unroll):
    b, _, _, s = qat.shape
    nkv, tk = vat.shape[2], vat.shape[4]
    chains = 2 * A_GROUP_Q // QBLK
    return pl.pallas_call(
        functools.partial(_attn_a_kernel, unroll=unroll),
        grid=(b, A_HEADS, s // tq),
        in_specs=[
            _const_spec((4, A_HEAD_DIM)), _const_spec((1, A_V_DIM)),
            pl.BlockSpec((1, 1, LANES, tq), lambda bi, h, i: (bi, h, 0, i)),
            pl.BlockSpec((1, 1, s, LANES), lambda bi, h, i: (bi, h, 0, 0)),
            pl.BlockSpec((1, 1, nkv, VA_ROWS, tk), lambda bi, h, i: (bi, h, 0, 0, 0)),
        ],
        out_specs=pl.BlockSpec((1, tq, A_V_DIM), lambda bi, h, i: (bi, i, h)),
        out_shape=jax.ShapeDtypeStruct((b, s, A_HEADS * A_V_DIM), BF16),
        scratch_shapes=[pltpu.VMEM((chains, 2, tk, QBLK), F32),
                        pltpu.VMEM((tq // A_GROUP_Q, chains, VA_ROWS, QBLK), F32)],
        compiler_params=pltpu.CompilerParams(
            dimension_semantics=("parallel", "parallel", "arbitrary"), vmem_limit_bytes=VMEM_LIMIT),
        name="attn_a",
    )(lam4, sg, qat, ka, vat)


def _attn_b_kernel(qt_ref, k_ref, vt_ref, o_ref, s_ref, acc_ref, *, unroll):
    tq = qt_ref.shape[4]
    nkv, tk = vt_ref.shape[2], vt_ref.shape[4]
    dv = B_HEAD_DIM
    row = lax.broadcasted_iota(jnp.int32, (LANES, QBLK), 0)
    first = pl.program_id(2) * HALF
    keep = (row >= first) & (row < first + HALF)
    for grp in range(tq // QBLK):
        acc = acc_ref.at[grp]
        q0 = grp * QBLK
        qs = []
        for g in range(B_GROUP):
            qt = qt_ref[0, 0, g, :, q0:q0 + QBLK]
            qs.append(jnp.where(keep, qt, jnp.zeros_like(qt)))
        _flash_attention(lambda j: k_ref[0, 0, _key_rows(j, tk), :], lambda j, c: vt_ref[0, 0, j],
                         qs, s_ref, acc, nkv, unroll)
        for blk in range(B_GROUP // 2):
            parts = [acc[c, :dv] / acc[c, dv:dv + 1] for c in (2 * blk, 2 * blk + 1)]
            o_ref[0, q0:q0 + QBLK, blk * LANES:(blk + 1) * LANES] = (
                jnp.concatenate(parts, axis=0).T.astype(BF16))


def _attn_b(qbt, kb, vbt, *, tq, unroll):
    b, _, _, _, s = qbt.shape
    nkv, tk = vbt.shape[2], vbt.shape[4]
    width = B_GROUP * B_HEAD_DIM
    rows = B_HEAD_DIM + ONES_ROWS
    return pl.pallas_call(
        functools.partial(_attn_b_kernel, unroll=unroll),
        grid=(b, B_PAIRS, 2, s // tq),
        in_specs=[
            pl.BlockSpec((1, 1, B_GROUP, LANES, tq), lambda bi, j, hf, i: (bi, j, 0, 0, i)),
            pl.BlockSpec((1, 1, s, LANES), lambda bi, j, hf, i: (bi, j, 0, 0)),
            pl.BlockSpec((1, 1, nkv, rows, tk), lambda bi, j, hf, i: (bi, j, 0, hf, 0)),
        ],
        out_specs=pl.BlockSpec((1, tq, width), lambda bi, j, hf, i: (bi, i, 2 * j + hf)),
        out_shape=jax.ShapeDtypeStruct((b, s, B_HEADS * B_HEAD_DIM), BF16),
        scratch_shapes=[pltpu.VMEM((B_GROUP, 2, tk, QBLK), F32),
                        pltpu.VMEM((tq // QBLK, B_GROUP, rows, QBLK), F32)],
        compiler_params=pltpu.CompilerParams(
            dimension_semantics=("parallel", "parallel", "parallel", "arbitrary"), vmem_limit_bytes=VMEM_LIMIT),
        name="attn_b",
    )(qbt, kb, vbt)


def _merge_kernel(x_ref, oa_ref, ob_ref, sig_ref, wa_ref, wb_ref, wo_ref, o_ref):
    pa = jnp.dot(oa_ref[0], wa_ref[...], preferred_element_type=F32)
    pb = jnp.dot(ob_ref[0], wb_ref[...], preferred_element_type=F32)
    mix = sig_ref[0, :, :D_MODEL] * pa + sig_ref[0, :, D_MODEL:] * pb
    o_ref[0] = x_ref[0] + jnp.dot(mix.astype(BF16), wo_ref[...], preferred_element_type=F32)


def _merge(x, oa, ob, sig, wa, wb, wo, *, tm):
    b, s, _ = x.shape
    rows = lambda width: pl.BlockSpec((1, tm, width), lambda bi, i: (bi, i, 0))
    wspec = _const_spec((D_MODEL, D_MODEL))
    return pl.pallas_call(
        _merge_kernel,
        grid=(b, s // tm),
        in_specs=[rows(D_MODEL), rows(D_MODEL), rows(D_MODEL), rows(GATE_W), wspec, wspec, wspec],
        out_specs=rows(D_MODEL),
        out_shape=jax.ShapeDtypeStruct((b, s, D_MODEL), F32),
        compiler_params=pltpu.CompilerParams(
            dimension_semantics=("parallel", "parallel"), vmem_limit_bytes=VMEM_LIMIT),
        name="merge",
    )(x, oa, ob, sig, wa, wb, wo)


def _ffn_kernel(xp_ref, x_ref, xn_ref, g_ref, wu_ref, cw_ref, cb_ref, wd_ref, gf_ref, o_ref, h_ref, *, ck):
    tm = x_ref.shape[1]
    i = pl.program_id(1)
    g = g_ref[...]
    x = x_ref[0]
    hp = jnp.where(i > 0, _rms_rows(xp_ref[0], g), 0.0)
    hn = jnp.where(i < pl.num_programs(1) - 1, _rms_rows(xn_ref[0], g), 0.0)
    h_ref[0:HALO] = hp.astype(BF16)
    h_ref[HALO:HALO + tm] = _rms_rows(x, g).astype(BF16)
    h_ref[HALO + tm:] = hn.astype(BF16)
    h = h_ref[...]

    def conv(u, c0):
        w = cw_ref[:, c0:c0 + ck]
        rows = u.shape[0]
        prev = pltpu.roll(u, 1, 0)[HALO:HALO + tm]
        nxt = pltpu.roll(u, rows - 1, 0)[HALO:HALO + tm]
        return prev * w[0:1] + u[HALO:HALO + tm] * w[1:2] + nxt * w[2:3] + cb_ref[:, c0:c0 + ck]

    def up(c):
        return (jnp.dot(h, wu_ref[:, c * ck:(c + 1) * ck], preferred_element_type=F32),
                jnp.dot(h, wu_ref[:, D_FF + c * ck:D_FF + (c + 1) * ck], preferred_element_type=F32))

    nc = D_FF // ck
    acc = jnp.zeros((tm, D_MODEL), F32)
    u_next = up(0)
    for c in range(nc):
        uv, ug = u_next
        if c + 1 < nc:
            u_next = up(c + 1)
        act = (jax.nn.silu(conv(ug, D_FF + c * ck)) * conv(uv, c * ck)).astype(BF16)
        acc = acc + jnp.dot(act, wd_ref[c * ck:(c + 1) * ck, :], preferred_element_type=F32)
    o_ref[0] = _rms_rows(x + acc, gf_ref[...])


def _ffn(x, g, wu, cw, cb, wd, gf, *, tm, ck):
    b, s, _ = x.shape
    nh = tm // HALO
    last = s // HALO - 1
    return pl.pallas_call(
        functools.partial(_ffn_kernel, ck=ck),
        grid=(b, s // tm),
        in_specs=[
            pl.BlockSpec((1, HALO, D_MODEL), lambda bi, i: (bi, jnp.maximum(i * nh - 1, 0), 0)),
            pl.BlockSpec((1, tm, D_MODEL), lambda bi, i: (bi, i, 0)),
            pl.BlockSpec((1, HALO, D_MODEL), lambda bi, i: (bi, jnp.minimum((i + 1) * nh, last), 0)),
            _const_spec((1, D_MODEL)),
            _const_spec((D_MODEL, 2 * D_FF)),
            _const_spec((3, 2 * D_FF)),
            _const_spec((1, 2 * D_FF)),
            _const_spec((D_FF, D_MODEL)),
            _const_spec((1, D_MODEL)),
        ],
        out_specs=pl.BlockSpec((1, tm, D_MODEL), lambda bi, i: (bi, i, 0)),
        out_shape=jax.ShapeDtypeStruct((b, s, D_MODEL), F32),
        scratch_shapes=[pltpu.VMEM((tm + 2 * HALO, D_MODEL), BF16)],
        compiler_params=pltpu.CompilerParams(
            dimension_semantics=("parallel", "parallel"), vmem_limit_bytes=VMEM_LIMIT),
        name="ffn",
    )(x, x, x, g, wu, cw, cb, wd, gf)


def _rope_angles(pos, dim, theta):
    inv = 1.0 / (theta ** (jnp.arange(0, dim, 2, dtype=F32) / dim))
    ang = pos.astype(F32)[:, None] * inv[None, :]
    return jnp.cos(ang), jnp.sin(ang)


def _rope_tables(s):
    pos = jnp.arange(s)
    ca, sa = _rope_angles(pos, A_ROT_DIM, A_ROPE_THETA)
    ones = jnp.ones((s, A_HEAD_DIM - A_ROT_DIM), F32)
    cos_a = jnp.concatenate([ca, ca, ones], axis=1)
    sin_a = jnp.concatenate([-sa, sa, 0.0 * ones], axis=1)
    half = B_HEAD_DIM // 2
    cr, sr = _rope_angles(pos // GRID_W, half, B_ROPE_THETA)
    cc, sc = _rope_angles(pos % GRID_W, half, B_ROPE_THETA)
    cos_b = jnp.concatenate([cr, cr, cc, cc], axis=1)
    sin_b = jnp.concatenate([-sr, sr, -sc, sc], axis=1)
    two = lambda t: jnp.concatenate([t, t], axis=1)
    return two(cos_a), two(sin_a), two(cos_b), two(sin_b)


def _w_in_columns():
    cols = list(range(C_QB))
    for j in range(B_PAIRS):
        for g in range(B_GROUP):
            for half in range(2):
                head = (2 * j + half) * B_GROUP + g
                cols.extend(range(C_QB + head * B_HEAD_DIM, C_QB + (head + 1) * B_HEAD_DIM))
    cols.extend(range(C_KB, D_IN))
    return jnp.asarray(cols, dtype=jnp.int32)


def _tile(n, pref):
    return pref if n % pref == 0 else n


def _trunk(x, p, tabs):
    s = x.shape[1]
    tk = 512
    nkv = s // tk
    qat, ka, vat, qbt, kb, vbt, sig = _inproj(x, p["norm_mix_g"], p["w_in"], tabs, p["q_norm_g"], p["k_norm_g"],
                                              tm=512, tk=tk)
    unroll = 8 if nkv % 8 == 0 else 2
    oa = _attn_a(p["lam4"], p["subln_g"], qat, ka, vat, tq=4 * A_GROUP_Q, unroll=unroll)
    ob = _attn_b(qbt, kb, vbt, tq=4 * QBLK, unroll=unroll)
    x1 = _merge(x, oa, ob, sig, p["w_proj_a"], p["w_proj_b"], p["w_out"], tm=_tile(s, 512))
    return _ffn(x1, p["norm_ffn_g"], p["w_up"], p["conv_w"], p["conv_b"], p["w_down"], p["norm_final_g"],
                tm=_tile(s, 1024), ck=256)


def kernel(x_prompt, x_sample, norm_mix_g, w_in, lam_q1, lam_k1, lam_q2, lam_k2, subln_g, q_norm_g, k_norm_g,
           w_proj_a, w_proj_b, w_out, norm_ffn_g, w_up, conv_w, conv_b, w_down, norm_final_g):
    two = lambda t: jnp.concatenate([t, t]).reshape(1, LANES).astype(F32)
    p = {
        "norm_mix_g": norm_mix_g[0].reshape(1, D_MODEL),
        "w_in": jnp.take(w_in[0], _w_in_columns(), axis=1).astype(BF16),
        "lam4": jnp.stack([lam_q1[0], lam_k1[0], lam_q2[0], lam_k2[0]]).astype(F32),
        "subln_g": subln_g[0].reshape(1, A_V_DIM),
        "q_norm_g": two(q_norm_g[0]),
        "k_norm_g": two(k_norm_g[0]),
        "w_proj_a": w_proj_a[0].astype(BF16),
        "w_proj_b": w_proj_b[0].astype(BF16),
        "w_out": w_out[0].astype(BF16),
        "norm_ffn_g": norm_ffn_g[0].reshape(1, D_MODEL),
        "w_up": w_up[0].astype(BF16),
        "conv_w": conv_w[0],
        "conv_b": conv_b[0].reshape(1, 2 * D_FF),
        "w_down": w_down[0].astype(BF16),
        "norm_final_g": norm_final_g.reshape(1, D_MODEL),
    }
    tables = {}
    outs = []
    for x in (x_prompt, x_sample):
        s = x.shape[1]
        if s not in tables:
            tables[s] = _rope_tables(s)
        outs.append(_trunk(x, p, tables[s]))
    return tuple(outs)
```

```python
import functools
import math

import jax
import jax.numpy as jnp
from jax import lax
from jax.experimental import pallas as pl
from jax.experimental.pallas import tpu as pltpu

D_MODEL = 1024
GRID_W = 64
EPS = 1e-6
A_HEADS = 8
A_HEAD_DIM = 64
A_V_DIM = 128
A_ROT_DIM = 16
A_ROPE_THETA = 500000.0
B_HEADS = 16
B_KV_HEADS = 4
B_GROUP = 4
B_HEAD_DIM = 64
B_ROPE_THETA = 10000.0
D_FF = 2816
LAM_INIT = 0.8 - 0.6 * math.exp(-0.3 * 0)

A_Q = A_HEADS * 2 * A_HEAD_DIM
A_K = A_Q
A_V = A_HEADS * A_V_DIM
B_Q = B_HEADS * B_HEAD_DIM
B_K = B_KV_HEADS * B_HEAD_DIM
B_V = B_K
GATE_W = 2 * D_MODEL
D_IN = A_Q + A_K + A_V + B_Q + B_K + B_V + GATE_W
C_QA = 0
C_KA = C_QA + A_Q
C_VA = C_KA + A_K
C_QB = C_VA + A_V
C_KB = C_QB + B_Q
C_VB = C_KB + B_K
C_GATE = C_VB + B_V

LANES = 128
HALF = 64
B_PAIRS = B_KV_HEADS // 2
HALO = 16
QBLK = 256
A_GROUP_Q = 2 * QBLK
ONES_ROWS = 16
VA_ROWS = A_V_DIM + ONES_ROWS
VB_ROWS = 2 * (B_HEAD_DIM + ONES_ROWS)
VMEM_LIMIT = 56 * 1024 * 1024

F32 = jnp.float32
BF16 = jnp.bfloat16


def _rms_rows(x, g):
    return x * lax.rsqrt(jnp.mean(x * x, axis=-1, keepdims=True) + EPS) * g


def _const_spec(shape):
    zeros = (0,) * len(shape)
    return pl.BlockSpec(shape, lambda *_: zeros, pipeline_mode=pl.Buffered(1))


def _inproj_kernel(x_ref, g_ref, wt_ref, wn_ref, ca_ref, sa_ref, cb_ref, sb_ref,
                     cat_ref, sat_ref, cbt_ref, sbt_ref, qgt_ref, kg_ref,
                     qat_ref, ka_ref, vat_ref, qbt_ref, kb_ref, vbt_ref, sig_ref, ht_ref):
    tm = x_ref.shape[1]
    hn = _rms_rows(x_ref[0], g_ref[...])
    h = hn.astype(BF16)
    for blk in range(D_MODEL // LANES):
        ht_ref[blk * LANES:(blk + 1) * LANES, :] = hn[:, blk * LANES:(blk + 1) * LANES].T.astype(BF16)
    ht = ht_ref[...]
    lane = lax.broadcasted_iota(jnp.int32, (tm, LANES), 1)
    lo_half = lane < HALF
    first8 = (lane & 8) == 0
    first16 = (lane & 16) == 0
    ca, sa, cb, sb = ca_ref[...], sa_ref[...], cb_ref[...], sb_ref[...]
    cat, sat, cbt, sbt = cat_ref[...], sat_ref[...], cbt_ref[...], sbt_ref[...]
    qgt, kg = qgt_ref[...], kg_ref[...]
    scale = A_HEAD_DIM ** -0.5 * math.log2(math.e)
    ones_rows = jnp.ones((ONES_ROWS, tm), BF16)

    def swap_rows(t, width):
        parts = [t[r:r + width] for r in range(0, t.shape[0], width)]
        return jnp.concatenate([parts[i ^ 1] for i in range(len(parts))], axis=0)

    def rope_a_t(t):
        return t * cat + swap_rows(t, 8) * sat

    def rope_b_t(t):
        return t * cbt + swap_rows(t, 16) * sbt

    def headnorm_t(t, g):
        halves = []
        for r0 in (0, HALF):
            th = t[r0:r0 + HALF]
            ms = jnp.mean(th * th, axis=0, keepdims=True)
            halves.append(th * lax.rsqrt(ms + EPS) * g[r0:r0 + HALF])
        return jnp.concatenate(halves, axis=0)

    def tproj(r0, n):
        return jnp.dot(wt_ref[r0:r0 + n, :], ht, preferred_element_type=F32)

    acc = tproj(0, A_Q)
    for hd in range(A_HEADS):
        qat_ref[0, hd] = (rope_a_t(acc[hd * LANES:(hd + 1) * LANES]) * scale).astype(BF16)
    acc = tproj(A_Q, A_V)
    for hd in range(A_HEADS):
        vat_ref[0, hd, 0, :A_V_DIM] = acc[hd * LANES:(hd + 1) * LANES].astype(BF16)
        vat_ref[0, hd, 0, A_V_DIM:] = ones_rows
    acc = tproj(A_Q + A_V, B_Q)
    for blk in range(B_PAIRS * B_GROUP):
        t = rope_b_t(headnorm_t(acc[blk * LANES:(blk + 1) * LANES], qgt)) * scale
        qbt_ref[0, blk // B_GROUP, blk % B_GROUP] = t.astype(BF16)
    acc = tproj(A_Q + A_V + B_Q, B_V)
    for j in range(B_PAIRS):
        for half in range(2):
            r0 = half * (B_HEAD_DIM + ONES_ROWS)
            src = j * LANES + half * B_HEAD_DIM
            vbt_ref[0, j, 0, r0:r0 + B_HEAD_DIM] = acc[src:src + B_HEAD_DIM].astype(BF16)
            vbt_ref[0, j, 0, r0 + B_HEAD_DIM:r0 + B_HEAD_DIM + ONES_ROWS] = ones_rows

    def proj(c0, n):
        return jnp.dot(h, wn_ref[:, c0:c0 + n], preferred_element_type=F32)

    def rope_a(t):
        partner = jnp.where(first8, pltpu.roll(t, LANES - 8, 1), pltpu.roll(t, 8, 1))
        return t * ca + partner * sa

    def rope_b(t):
        partner = jnp.where(first16, pltpu.roll(t, LANES - 16, 1), pltpu.roll(t, 16, 1))
        return t * cb + partner * sb

    def headnorm(t, g):
        sq = t * t
        s_lo = jnp.sum(jnp.where(lo_half, sq, 0.0), axis=1, keepdims=True)
        s_hi = jnp.sum(jnp.where(lo_half, 0.0, sq), axis=1, keepdims=True)
        ms = jnp.where(lo_half, s_lo, s_hi) * (1.0 / HALF)
        return t * lax.rsqrt(ms + EPS) * g

    for p in range(A_HEADS // 2):
        acc = proj(2 * LANES * p, 2 * LANES)
        for i in range(2):
            ka_ref[0, 2 * p + i] = rope_a(acc[:, i * LANES:(i + 1) * LANES]).astype(BF16)
    acc = proj(A_K, B_K)
    for j in range(B_PAIRS):
        kb_ref[0, j] = rope_b(headnorm(acc[:, j * LANES:(j + 1) * LANES], kg)).astype(BF16)
    for p in range(GATE_W // (2 * LANES)):
        acc = proj(A_K + B_K + 2 * LANES * p, 2 * LANES)
        sig_ref[0, :, 2 * LANES * p:2 * LANES * (p + 1)] = jax.nn.sigmoid(acc).astype(sig_ref.dtype)


def _inproj(x, g, wt, wn, tabs, tabs_t, qgt, kg, *, tm, tk):
    b, s, _ = x.shape
    nt = s // tk
    sub = tk // tm
    row = lambda bi, i: (i, 0)
    col = lambda bi, i: (0, i)
    out_shapes = (
        jax.ShapeDtypeStruct((b, A_HEADS, LANES, s), BF16),
        jax.ShapeDtypeStruct((b, A_HEADS, s, LANES), BF16),
        jax.ShapeDtypeStruct((b, A_HEADS, nt, VA_ROWS, tk), BF16),
        jax.ShapeDtypeStruct((b, B_PAIRS, B_GROUP, LANES, s), BF16),
        jax.ShapeDtypeStruct((b, B_PAIRS, s, LANES), BF16),
        jax.ShapeDtypeStruct((b, B_PAIRS, nt, VB_ROWS, tk), BF16),
        jax.ShapeDtypeStruct((b, s, GATE_W), BF16),
    )
    out_specs = (
        pl.BlockSpec((1, A_HEADS, LANES, tm), lambda bi, i: (bi, 0, 0, i)),
        pl.BlockSpec((1, A_HEADS, tm, LANES), lambda bi, i: (bi, 0, i, 0)),
        pl.BlockSpec((1, A_HEADS, 1, VA_ROWS, tm), lambda bi, i: (bi, 0, i // sub, 0, i % sub)),
        pl.BlockSpec((1, B_PAIRS, B_GROUP, LANES, tm), lambda bi, i: (bi, 0, 0, 0, i)),
        pl.BlockSpec((1, B_PAIRS, tm, LANES), lambda bi, i: (bi, 0, i, 0)),
        pl.BlockSpec((1, B_PAIRS, 1, VB_ROWS, tm), lambda bi, i: (bi, 0, i // sub, 0, i % sub)),
        pl.BlockSpec((1, tm, GATE_W), lambda bi, i: (bi, i, 0)),
    )
    in_specs = [
        pl.BlockSpec((1, tm, D_MODEL), lambda bi, i: (bi, i, 0)),
        _const_spec((1, D_MODEL)),
        _const_spec(wt.shape),
        _const_spec(wn.shape),
        pl.BlockSpec((tm, LANES), row), pl.BlockSpec((tm, LANES), row),
        pl.BlockSpec((tm, LANES), row), pl.BlockSpec((tm, LANES), row),
        pl.BlockSpec((LANES, tm), col), pl.BlockSpec((LANES, tm), col),
        pl.BlockSpec((LANES, tm), col), pl.BlockSpec((LANES, tm), col),
        _const_spec((LANES, tm)), _const_spec((1, LANES)),
    ]
    return pl.pallas_call(
        _inproj_kernel, grid=(b, s // tm), in_specs=in_specs, out_specs=out_specs, out_shape=out_shapes,
        scratch_shapes=[pltpu.VMEM((D_MODEL, tm), BF16)],
        compiler_params=pltpu.CompilerParams(
            dimension_semantics=("parallel", "parallel"), vmem_limit_bytes=VMEM_LIMIT),
        name="inproj",
    )(x, g, wt, wn, *tabs, *tabs_t, qgt, kg)


def _flash_attention(k_tile, vt_tile, qs, s_ref, acc_ref, nkv, unroll):
    n = len(qs)
    assert unroll % 2 == 0 and nkv % unroll == 0

    def score(k, c, slot):
        s = jnp.dot(k, qs[c], preferred_element_type=F32)
        s_ref[c, slot] = s
        return jnp.max(s, axis=0, keepdims=True)

    def scores(j, slot):
        k = k_tile(j)
        return tuple(score(k, c, slot) for c in range(n))

    def update(j, c, slot, mx, m):
        m_new = jnp.maximum(m, mx)
        alpha = jnp.exp2(m - m_new)
        p = jnp.exp2(s_ref[c, slot] - m_new).astype(BF16)
        acc_ref[c] = alpha * acc_ref[c] + jnp.dot(vt_tile(j, c), p, preferred_element_type=F32)
        return m_new

    acc_ref[...] = jnp.zeros(acc_ref.shape, F32)
    tq = qs[0].shape[1]
    m0 = (jnp.full((1, tq), -jnp.inf, F32),) * n

    def tiles(j0, m, mx, last):
        m = list(m)
        for t in range(unroll):
            skip = last and t == unroll - 1
            k = None if skip else k_tile(j0 + t + 1)
            mx_next = []
            for c in range(n):
                if not skip:
                    mx_next.append(score(k, c, (t + 1) % 2))
                m[c] = update(j0 + t, c, t % 2, mx[c], m[c])
            mx = tuple(mx_next)
        return tuple(m), mx

    carry = (m0, scores(0, 0))
    carry = lax.fori_loop(0, nkv // unroll - 1, lambda i, c: tiles(i * unroll, *c, False), carry)
    tiles(nkv - unroll, *carry, True)


def _key_rows(j, tk):
    start = j * tk
    return pl.ds(start if isinstance(start, int) else pl.multiple_of(start, tk), tk)


def _attn_a_kernel(lam_ref, sg_ref, qt_ref, k_ref, vt_ref, o_ref, s_ref, acc_ref, *, unroll):
    tq = qt_ref.shape[3]
    nkv, tk = vt_ref.shape[2], vt_ref.shape[4]
    row = lax.broadcasted_iota(jnp.int32, (LANES, QBLK), 0)
    lam_v = lam_ref[...]
    lam = (jnp.exp(jnp.sum(lam_v[0:1] * lam_v[1:2], axis=1, keepdims=True))
           - jnp.exp(jnp.sum(lam_v[2:3] * lam_v[3:4], axis=1, keepdims=True)) + LAM_INIT)
    for grp in range(tq // A_GROUP_Q):
        acc = acc_ref.at[grp]
        blocks = [grp * A_GROUP_Q + i * QBLK for i in range(A_GROUP_Q // QBLK)]
        qs = []
        for q0 in blocks:
            qt = qt_ref[0, 0, :, q0:q0 + QBLK]
            zero = jnp.zeros_like(qt)
            qs += [jnp.where(row < HALF, qt, zero), jnp.where(row >= HALF, qt, zero)]
        _flash_attention(
            lambda j: k_ref[0, 0, _key_rows(j, tk), :],
            lambda j, c: vt_ref[0, 0, j],
            qs, s_ref, acc, nkv, unroll)
        for i, q0 in enumerate(blocks):
            o1 = acc[2 * i, :A_V_DIM] / acc[2 * i, A_V_DIM:A_V_DIM + 1]
            o2 = acc[2 * i + 1, :A_V_DIM] / acc[2 * i + 1, A_V_DIM:A_V_DIM + 1]
            o = o1 - lam * o2
            y = _rms_rows(o.T, sg_ref[...]) * (1.0 - LAM_INIT)
            o_ref[0, q0:q0 + QBLK, :] = y.astype(BF16)


def _attn_a(lam4, sg, qat, ka, vat, *, tq, unroll):
    b, _, _, s = qat.shape
    nkv, tk = vat.shape[2], vat.shape[4]
    chains = 2 * A_GROUP_Q // QBLK
    return pl.pallas_call(
        functools.partial(_attn_a_kernel, unroll=unroll),
        grid=(b, A_HEADS, s // tq),
        in_specs=[
            _const_spec((4, A_HEAD_DIM)), _const_spec((1, A_V_DIM)),
            pl.BlockSpec((1, 1, LANES, tq), lambda bi, h, i: (bi, h, 0, i)),
            pl.BlockSpec((1, 1, s, LANES), lambda bi, h, i: (bi, h, 0, 0)),
            pl.BlockSpec((1, 1, nkv, VA_ROWS, tk), lambda bi, h, i: (bi, h, 0, 0, 0)),
        ],
        out_specs=pl.BlockSpec((1, tq, A_V_DIM), lambda bi, h, i: (bi, i, h)),
        out_shape=jax.ShapeDtypeStruct((b, s, A_HEADS * A_V_DIM), BF16),
        scratch_shapes=[pltpu.VMEM((chains, 2, tk, QBLK), F32),
                        pltpu.VMEM((tq // A_GROUP_Q, chains, VA_ROWS, QBLK), F32)],
        compiler_params=pltpu.CompilerParams(
            dimension_semantics=("parallel", "parallel", "arbitrary"), vmem_limit_bytes=VMEM_LIMIT),
        name="attn_a",
    )(lam4, sg, qat, ka, vat)


def _attn_b_kernel(qt_ref, k_ref, vt_ref, o_ref, s_ref, acc_ref, *, unroll):
    tq = qt_ref.shape[4]
    nkv, tk = vt_ref.shape[2], vt_ref.shape[4]
    dv = B_HEAD_DIM
    row = lax.broadcasted_iota(jnp.int32, (LANES, QBLK), 0)
    first = pl.program_id(2) * HALF
    keep = (row >= first) & (row < first + HALF)
    for grp in range(tq // QBLK):
        acc = acc_ref.at[grp]
        q0 = grp * QBLK
        qs = []
        for g in range(B_GROUP):
            qt = qt_ref[0, 0, g, :, q0:q0 + QBLK]
            qs.append(jnp.where(keep, qt, jnp.zeros_like(qt)))
        _flash_attention(lambda j: k_ref[0, 0, _key_rows(j, tk), :], lambda j, c: vt_ref[0, 0, j],
                         qs, s_ref, acc, nkv, unroll)
        for blk in range(B_GROUP // 2):
            parts = [acc[c, :dv] / acc[c, dv:dv + 1] for c in (2 * blk, 2 * blk + 1)]
            o_ref[0, q0:q0 + QBLK, blk * LANES:(blk + 1) * LANES] = (
                jnp.concatenate(parts, axis=0).T.astype(BF16))


def _attn_b(qbt, kb, vbt, *, tq, unroll):
    b, _, _, _, s = qbt.shape
    nkv, tk = vbt.shape[2], vbt.shape[4]
    width = B_GROUP * B_HEAD_DIM
    rows = B_HEAD_DIM + ONES_ROWS
    return pl.pallas_call(
        functools.partial(_attn_b_kernel, unroll=unroll),
        grid=(b, B_PAIRS, 2, s // tq),
        in_specs=[
            pl.BlockSpec((1, 1, B_GROUP, LANES, tq), lambda bi, j, hf, i: (bi, j, 0, 0, i)),
            pl.BlockSpec((1, 1, s, LANES), lambda bi, j, hf, i: (bi, j, 0, 0)),
            pl.BlockSpec((1, 1, nkv, rows, tk), lambda bi, j, hf, i: (bi, j, 0, hf, 0)),
        ],
        out_specs=pl.BlockSpec((1, tq, width), lambda bi, j, hf, i: (bi, i, 2 * j + hf)),
        out_shape=jax.ShapeDtypeStruct((b, s, B_HEADS * B_HEAD_DIM), BF16),
        scratch_shapes=[pltpu.VMEM((B_GROUP, 2, tk, QBLK), F32),
                        pltpu.VMEM((tq // QBLK, B_GROUP, rows, QBLK), F32)],
        compiler_params=pltpu.CompilerParams(
            dimension_semantics=("parallel", "parallel", "parallel", "arbitrary"), vmem_limit_bytes=VMEM_LIMIT),
        name="attn_b",
    )(qbt, kb, vbt)


def _merge_kernel(x_ref, oa_ref, ob_ref, sig_ref, wa_ref, wb_ref, wo_ref, o_ref):
    pa = jnp.dot(oa_ref[0], wa_ref[...], preferred_element_type=F32)
    pb = jnp.dot(ob_ref[0], wb_ref[...], preferred_element_type=F32)
    mix = sig_ref[0, :, :D_MODEL] * pa + sig_ref[0, :, D_MODEL:] * pb
    o_ref[0] = x_ref[0] + jnp.dot(mix.astype(BF16), wo_ref[...], preferred_element_type=F32)


def _merge(x, oa, ob, sig, wa, wb, wo, *, tm):
    b, s, _ = x.shape
    rows = lambda width: pl.BlockSpec((1, tm, width), lambda bi, i: (bi, i, 0))
    wspec = _const_spec((D_MODEL, D_MODEL))
    return pl.pallas_call(
        _merge_kernel,
        grid=(b, s // tm),
        in_specs=[rows(D_MODEL), rows(D_MODEL), rows(D_MODEL), rows(GATE_W), wspec, wspec, wspec],
        out_specs=rows(D_MODEL),
        out_shape=jax.ShapeDtypeStruct((b, s, D_MODEL), F32),
        compiler_params=pltpu.CompilerParams(
            dimension_semantics=("parallel", "parallel"), vmem_limit_bytes=VMEM_LIMIT),
        name="merge",
    )(x, oa, ob, sig, wa, wb, wo)


def _ffn_kernel(xp_ref, x_ref, xn_ref, g_ref, wu_ref, cw_ref, cb_ref, wd_ref, gf_ref, o_ref, h_ref, *, ck):
    tm = x_ref.shape[1]
    i = pl.program_id(1)
    g = g_ref[...]
    x = x_ref[0]
    hp = jnp.where(i > 0, _rms_rows(xp_ref[0], g), 0.0)
    hn = jnp.where(i < pl.num_programs(1) - 1, _rms_rows(xn_ref[0], g), 0.0)
    h_ref[0:HALO] = hp.astype(BF16)
    h_ref[HALO:HALO + tm] = _rms_rows(x, g).astype(BF16)
    h_ref[HALO + tm:] = hn.astype(BF16)
    h = h_ref[...]

    def conv(u, c0):
        w = cw_ref[:, c0:c0 + ck]
        rows = u.shape[0]
        prev = pltpu.roll(u, 1, 0)[HALO:HALO + tm]
        nxt = pltpu.roll(u, rows - 1, 0)[HALO:HALO + tm]
        return prev * w[0:1] + u[HALO:HALO + tm] * w[1:2] + nxt * w[2:3] + cb_ref[:, c0:c0 + ck]

    def up(c):
        return (jnp.dot(h, wu_ref[:, c * ck:(c + 1) * ck], preferred_element_type=F32),
                jnp.dot(h, wu_ref[:, D_FF + c * ck:D_FF + (c + 1) * ck], preferred_element_type=F32))

    nc = D_FF // ck
    acc = jnp.zeros((tm, D_MODEL), F32)
    u_next = up(0)
    for c in range(nc):
        uv, ug = u_next
        if c + 1 < nc:
            u_next = up(c + 1)
        act = (jax.nn.silu(conv(ug, D_FF + c * ck)) * conv(uv, c * ck)).astype(BF16)
        acc = acc + jnp.dot(act, wd_ref[c * ck:(c + 1) * ck, :], preferred_element_type=F32)
    o_ref[0] = _rms_rows(x + acc, gf_ref[...])


def _ffn(x, g, wu, cw, cb, wd, gf, *, tm, ck):
    b, s, _ = x.shape
    nh = tm // HALO
    last = s // HALO - 1
    return pl.pallas_call(
        functools.partial(_ffn_kernel, ck=ck),
        grid=(b, s // tm),
        in_specs=[
            pl.BlockSpec((1, HALO, D_MODEL), lambda bi, i: (bi, jnp.maximum(i * nh - 1, 0), 0)),
            pl.BlockSpec((1, tm, D_MODEL), lambda bi, i: (bi, i, 0)),
            pl.BlockSpec((1, HALO, D_MODEL), lambda bi, i: (bi, jnp.minimum((i + 1) * nh, last), 0)),
            _const_spec((1, D_MODEL)),
            _const_spec((D_MODEL, 2 * D_FF)),
            _const_spec((3, 2 * D_FF)),
            _const_spec((1, 2 * D_FF)),
            _const_spec((D_FF, D_MODEL)),
            _const_spec((1, D_MODEL)),
        ],
        out_specs=pl.BlockSpec((1, tm, D_MODEL), lambda bi, i: (bi, i, 0)),
        out_shape=jax.ShapeDtypeStruct((b, s, D_MODEL), F32),
        scratch_shapes=[pltpu.VMEM((tm + 2 * HALO, D_MODEL), BF16)],
        compiler_params=pltpu.CompilerParams(
            dimension_semantics=("parallel", "parallel"), vmem_limit_bytes=VMEM_LIMIT),
        name="ffn",
    )(x, x, x, g, wu, cw, cb, wd, gf)


def _rope_angles(pos, dim, theta):
    inv = 1.0 / (theta ** (jnp.arange(0, dim, 2, dtype=F32) / dim))
    ang = pos.astype(F32)[:, None] * inv[None, :]
    return jnp.cos(ang), jnp.sin(ang)


def _rope_tables(s):
    pos = jnp.arange(s)
    ca, sa = _rope_angles(pos, A_ROT_DIM, A_ROPE_THETA)
    ones = jnp.ones((s, A_HEAD_DIM - A_ROT_DIM), F32)
    cos_a = jnp.concatenate([ca, ca, ones], axis=1)
    sin_a = jnp.concatenate([-sa, sa, 0.0 * ones], axis=1)
    half = B_HEAD_DIM // 2
    cr, sr = _rope_angles(pos // GRID_W, half, B_ROPE_THETA)
    cc, sc = _rope_angles(pos % GRID_W, half, B_ROPE_THETA)
    cos_b = jnp.concatenate([cr, cr, cc, cc], axis=1)
    sin_b = jnp.concatenate([-sr, sr, -sc, sc], axis=1)
    two = lambda t: jnp.concatenate([t, t], axis=1)
    return two(cos_a), two(sin_a), two(cos_b), two(sin_b)


def _w_in_columns():
    cols = list(range(C_QB))
    for j in range(B_PAIRS):
        for g in range(B_GROUP):
            for half in range(2):
                head = (2 * j + half) * B_GROUP + g
                cols.extend(range(C_QB + head * B_HEAD_DIM, C_QB + (head + 1) * B_HEAD_DIM))
    cols.extend(range(C_KB, D_IN))
    return jnp.asarray(cols, dtype=jnp.int32)


_T_COLS = list(range(C_QA, C_KA)) + list(range(C_VA, C_KB)) + list(range(C_VB, C_GATE))
_N_COLS = list(range(C_KA, C_VA)) + list(range(C_KB, C_VB)) + list(range(C_GATE, D_IN))


def _tile(n, pref):
    return pref if n % pref == 0 else n


IN_TM = 512


def _trunk(x, p, tabs, tabs_t):
    s = x.shape[1]
    tk = 512
    nkv = s // tk
    qat, ka, vat, qbt, kb, vbt, sig = _inproj(x, p["norm_mix_g"], p["w_in_t"], p["w_in_n"], tabs, tabs_t,
                                              p["q_norm_gt"], p["k_norm_g"], tm=IN_TM, tk=tk)
    unroll = 8 if nkv % 8 == 0 else 2
    oa = _attn_a(p["lam4"], p["subln_g"], qat, ka, vat, tq=4 * A_GROUP_Q, unroll=unroll)
    ob = _attn_b(qbt, kb, vbt, tq=4 * QBLK, unroll=unroll)
    x1 = _merge(x, oa, ob, sig, p["w_proj_a"], p["w_proj_b"], p["w_out"], tm=_tile(s, 512))
    return _ffn(x1, p["norm_ffn_g"], p["w_up"], p["conv_w"], p["conv_b"], p["w_down"], p["norm_final_g"],
                tm=_tile(s, 1024), ck=256)


def kernel(x_prompt, x_sample, norm_mix_g, w_in, lam_q1, lam_k1, lam_q2, lam_k2, subln_g, q_norm_g, k_norm_g,
           w_proj_a, w_proj_b, w_out, norm_ffn_g, w_up, conv_w, conv_b, w_down, norm_final_g):
    two = lambda t: jnp.concatenate([t, t]).reshape(1, LANES).astype(F32)
    p = {
        "norm_mix_g": norm_mix_g[0].reshape(1, D_MODEL),
        "w_in_t": jnp.take(w_in[0], _w_in_columns()[jnp.asarray(_T_COLS)], axis=1).T.astype(BF16),
        "w_in_n": jnp.take(w_in[0], _w_in_columns()[jnp.asarray(_N_COLS)], axis=1).astype(BF16),
        "lam4": jnp.stack([lam_q1[0], lam_k1[0], lam_q2[0], lam_k2[0]]).astype(F32),
        "subln_g": subln_g[0].reshape(1, A_V_DIM),
        "q_norm_gt": jnp.broadcast_to(two(q_norm_g[0]).reshape(LANES, 1), (LANES, IN_TM)),
        "k_norm_g": two(k_norm_g[0]),
        "w_proj_a": w_proj_a[0].astype(BF16),
        "w_proj_b": w_proj_b[0].astype(BF16),
        "w_out": w_out[0].astype(BF16),
        "norm_ffn_g": norm_ffn_g[0].reshape(1, D_MODEL),
        "w_up": w_up[0].astype(BF16),
        "conv_w": conv_w[0],
        "conv_b": conv_b[0].reshape(1, 2 * D_FF),
        "w_down": w_down[0].astype(BF16),
        "norm_final_g": norm_final_g.reshape(1, D_MODEL),
    }
    tables = {}
    outs = []
    for x in (x_prompt, x_sample):
        s = x.shape[1]
        if s not in tables:
            tabs = _rope_tables(s)
            tables[s] = (tabs, tuple(t.T for t in tabs))
        outs.append(_trunk(x, p, *tables[s]))
    return tuple(outs)
```

```python
import functools
import math

import jax
import jax.numpy as jnp
from jax import lax
from jax.experimental import pallas as pl
from jax.experimental.pallas import tpu as pltpu

D_MODEL = 1024
GRID_W = 64
EPS = 1e-6
A_HEADS = 8
A_HEAD_DIM = 64
A_V_DIM = 128
A_ROT_DIM = 16
A_ROPE_THETA = 500000.0
B_HEADS = 16
B_KV_HEADS = 4
B_GROUP = 4
B_HEAD_DIM = 64
B_ROPE_THETA = 10000.0
D_FF = 2816
LAM_INIT = 0.8 - 0.6 * math.exp(-0.3 * 0)

A_Q = A_HEADS * 2 * A_HEAD_DIM
A_K = A_Q
A_V = A_HEADS * A_V_DIM
B_Q = B_HEADS * B_HEAD_DIM
B_K = B_KV_HEADS * B_HEAD_DIM
B_V = B_K
GATE_W = 2 * D_MODEL
D_IN = A_Q + A_K + A_V + B_Q + B_K + B_V + GATE_W
C_QA = 0
C_KA = C_QA + A_Q
C_VA = C_KA + A_K
C_QB = C_VA + A_V
C_KB = C_QB + B_Q
C_VB = C_KB + B_K
C_GATE = C_VB + B_V

LANES = 128
HALF = 64
B_PAIRS = B_KV_HEADS // 2
HALO = 16
QBLK = 256
A_GROUP_Q = 2 * QBLK
ONES_ROWS = 16
VA_ROWS = A_V_DIM + ONES_ROWS
VB_ROWS = 2 * (B_HEAD_DIM + ONES_ROWS)
VMEM_LIMIT = 56 * 1024 * 1024

F32 = jnp.float32
BF16 = jnp.bfloat16


def _rms_rows(x, g):
    return x * lax.rsqrt(jnp.mean(x * x, axis=-1, keepdims=True) + EPS) * g


def _const_spec(shape):
    zeros = (0,) * len(shape)
    return pl.BlockSpec(shape, lambda *_: zeros, pipeline_mode=pl.Buffered(1))


def _inproj_kernel(x_ref, g_ref, wt_ref, wn_ref, ca_ref, sa_ref, cb_ref, sb_ref,
                     cat_ref, sat_ref, cbt_ref, sbt_ref, qgt_ref, kg_ref,
                     qat_ref, ka_ref, vat_ref, qbt_ref, kb_ref, vbt_ref, sig_ref, ht_ref):
    tm = x_ref.shape[1]
    hn = _rms_rows(x_ref[0], g_ref[...])
    h = hn.astype(BF16)
    for blk in range(D_MODEL // LANES):
        ht_ref[blk * LANES:(blk + 1) * LANES, :] = hn[:, blk * LANES:(blk + 1) * LANES].T.astype(BF16)
    ht = ht_ref[...]
    lane = lax.broadcasted_iota(jnp.int32, (tm, LANES), 1)
    lo_half = lane < HALF
    first8 = (lane & 8) == 0
    first16 = (lane & 16) == 0
    ca, sa, cb, sb = ca_ref[...], sa_ref[...], cb_ref[...], sb_ref[...]
    cat, sat, cbt, sbt = cat_ref[...], sat_ref[...], cbt_ref[...], sbt_ref[...]
    qgt, kg = qgt_ref[...], kg_ref[...]
    scale = A_HEAD_DIM ** -0.5 * math.log2(math.e)
    ones_rows = jnp.ones((ONES_ROWS, tm), BF16)

    def swap_rows(t, width):
        parts = [t[r:r + width] for r in range(0, t.shape[0], width)]
        return jnp.concatenate([parts[i ^ 1] for i in range(len(parts))], axis=0)

    def rope_a_t(t):
        return t * cat + swap_rows(t, 8) * sat

    def rope_b_t(t):
        return t * cbt + swap_rows(t, 16) * sbt

    def headnorm_t(t, g):
        halves = []
        for r0 in (0, HALF):
            th = t[r0:r0 + HALF]
            ms = jnp.mean(th * th, axis=0, keepdims=True)
            halves.append(th * lax.rsqrt(ms + EPS) * g[r0:r0 + HALF])
        return jnp.concatenate(halves, axis=0)

    def tproj(r0, n):
        return jnp.dot(wt_ref[r0:r0 + n, :], ht, preferred_element_type=F32)

    acc = tproj(0, A_Q)
    for hd in range(A_HEADS):
        qat_ref[0, hd] = (rope_a_t(acc[hd * LANES:(hd + 1) * LANES]) * scale).astype(BF16)
    acc = tproj(A_Q, A_V)
    for hd in range(A_HEADS):
        vat_ref[0, hd, 0, :A_V_DIM] = acc[hd * LANES:(hd + 1) * LANES].astype(BF16)
        vat_ref[0, hd, 0, A_V_DIM:] = ones_rows
    acc = tproj(A_Q + A_V, B_Q)
    for blk in range(B_PAIRS * B_GROUP):
        t = rope_b_t(headnorm_t(acc[blk * LANES:(blk + 1) * LANES], qgt)) * scale
        qbt_ref[0, blk // B_GROUP, blk % B_GROUP] = t.astype(BF16)
    acc = tproj(A_Q + A_V + B_Q, B_V)
    for j in range(B_PAIRS):
        for half in range(2):
            r0 = half * (B_HEAD_DIM + ONES_ROWS)
            src = j * LANES + half * B_HEAD_DIM
            vbt_ref[0, j, 0, r0:r0 + B_HEAD_DIM] = acc[src:src + B_HEAD_DIM].astype(BF16)
            vbt_ref[0, j, 0, r0 + B_HEAD_DIM:r0 + B_HEAD_DIM + ONES_ROWS] = ones_rows

    def proj(c0, n):
        return jnp.dot(h, wn_ref[:, c0:c0 + n], preferred_element_type=F32)

    def rope_a(t):
        partner = jnp.where(first8, pltpu.roll(t, LANES - 8, 1), pltpu.roll(t, 8, 1))
        return t * ca + partner * sa

    def rope_b(t):
        partner = jnp.where(first16, pltpu.roll(t, LANES - 16, 1), pltpu.roll(t, 16, 1))
        return t * cb + partner * sb

    def headnorm(t, g):
        sq = t * t
        s_lo = jnp.sum(jnp.where(lo_half, sq, 0.0), axis=1, keepdims=True)
        s_hi = jnp.sum(jnp.where(lo_half, 0.0, sq), axis=1, keepdims=True)
        ms = jnp.where(lo_half, s_lo, s_hi) * (1.0 / HALF)
        return t * lax.rsqrt(ms + EPS) * g

    for p in range(A_HEADS // 2):
        acc = proj(2 * LANES * p, 2 * LANES)
        for i in range(2):
            ka_ref[0, 2 * p + i] = rope_a(acc[:, i * LANES:(i + 1) * LANES]).astype(BF16)
    acc = proj(A_K, B_K)
    for j in range(B_PAIRS):
        kb_ref[0, j] = rope_b(headnorm(acc[:, j * LANES:(j + 1) * LANES], kg)).astype(BF16)
    for p in range(GATE_W // (2 * LANES)):
        acc = proj(A_K + B_K + 2 * LANES * p, 2 * LANES)
        sig_ref[0, :, 2 * LANES * p:2 * LANES * (p + 1)] = jax.nn.sigmoid(acc).astype(sig_ref.dtype)


def _inproj(x, g, wt, wn, tabs, tabs_t, qgt, kg, *, tm, tk):
    b, s, _ = x.shape
    nt = s // tk
    sub = tk // tm
    row = lambda bi, i: (i, 0)
    col = lambda bi, i: (0, i)
    out_shapes = (
        jax.ShapeDtypeStruct((b, A_HEADS, LANES, s), BF16),
        jax.ShapeDtypeStruct((b, A_HEADS, s, LANES), BF16),
        jax.ShapeDtypeStruct((b, A_HEADS, nt, VA_ROWS, tk), BF16),
        jax.ShapeDtypeStruct((b, B_PAIRS, B_GROUP, LANES, s), BF16),
        jax.ShapeDtypeStruct((b, B_PAIRS, s, LANES), BF16),
        jax.ShapeDtypeStruct((b, B_PAIRS, nt, VB_ROWS, tk), BF16),
        jax.ShapeDtypeStruct((b, s, GATE_W), BF16),
    )
    out_specs = (
        pl.BlockSpec((1, A_HEADS, LANES, tm), lambda bi, i: (bi, 0, 0, i)),
        pl.BlockSpec((1, A_HEADS, tm, LANES), lambda bi, i: (bi, 0, i, 0)),
        pl.BlockSpec((1, A_HEADS, 1, VA_ROWS, tm), lambda bi, i: (bi, 0, i // sub, 0, i % sub)),
        pl.BlockSpec((1, B_PAIRS, B_GROUP, LANES, tm), lambda bi, i: (bi, 0, 0, 0, i)),
        pl.BlockSpec((1, B_PAIRS, tm, LANES), lambda bi, i: (bi, 0, i, 0)),
        pl.BlockSpec((1, B_PAIRS, 1, VB_ROWS, tm), lambda bi, i: (bi, 0, i // sub, 0, i % sub)),
        pl.BlockSpec((1, tm, GATE_W), lambda bi, i: (bi, i, 0)),
    )
    in_specs = [
        pl.BlockSpec((1, tm, D_MODEL), lambda bi, i: (bi, i, 0)),
        _const_spec((1, D_MODEL)),
        _const_spec(wt.shape),
        _const_spec(wn.shape),
        pl.BlockSpec((tm, LANES), row), pl.BlockSpec((tm, LANES), row),
        pl.BlockSpec((tm, LANES), row), pl.BlockSpec((tm, LANES), row),
        pl.BlockSpec((LANES, tm), col), pl.BlockSpec((LANES, tm), col),
        pl.BlockSpec((LANES, tm), col), pl.BlockSpec((LANES, tm), col),
        _const_spec((LANES, tm)), _const_spec((1, LANES)),
    ]
    return pl.pallas_call(
        _inproj_kernel, grid=(b, s // tm), in_specs=in_specs, out_specs=out_specs, out_shape=out_shapes,
        scratch_shapes=[pltpu.VMEM((D_MODEL, tm), BF16)],
        compiler_params=pltpu.CompilerParams(
            dimension_semantics=("parallel", "parallel"), vmem_limit_bytes=VMEM_LIMIT),
        name="inproj",
    )(x, g, wt, wn, *tabs, *tabs_t, qgt, kg)


def _flash_attention(k_tile, vt_tile, qs, s_ref, acc_ref, nkv, unroll):
    n = len(qs)
    assert unroll % 2 == 0 and nkv % unroll == 0

    def score(k, c, slot):
        s = jnp.dot(k, qs[c], preferred_element_type=F32)
        s_ref[c, slot] = s
        return jnp.max(s, axis=0, keepdims=True)

    def scores(j, slot):
        k = k_tile(j)
        return tuple(score(k, c, slot) for c in range(n))

    def update(j, c, slot, mx, m):
        m_new = jnp.maximum(m, mx)
        alpha = jnp.exp2(m - m_new)
        p = jnp.exp2(s_ref[c, slot] - m_new).astype(BF16)
        acc_ref[c] = alpha * acc_ref[c] + jnp.dot(vt_tile(j, c), p, preferred_element_type=F32)
        return m_new

    acc_ref[...] = jnp.zeros(acc_ref.shape, F32)
    tq = qs[0].shape[1]
    m0 = (jnp.full((1, tq), -jnp.inf, F32),) * n

    def tiles(j0, m, mx, last):
        m = list(m)
        for t in range(unroll):
            skip = last and t == unroll - 1
            k = None if skip else k_tile(j0 + t + 1)
            mx_next = []
            for c in range(n):
                if not skip:
                    mx_next.append(score(k, c, (t + 1) % 2))
                m[c] = update(j0 + t, c, t % 2, mx[c], m[c])
            mx = tuple(mx_next)
        return tuple(m), mx

    carry = (m0, scores(0, 0))
    carry = lax.fori_loop(0, nkv // unroll - 1, lambda i, c: tiles(i * unroll, *c, False), carry)
    tiles(nkv - unroll, *carry, True)


def _key_rows(j, tk):
    start = j * tk
    return pl.ds(start if isinstance(start, int) else pl.multiple_of(start, tk), tk)


def _attn_a_kernel(lam_ref, sg_ref, qt_ref, k_ref, vt_ref, o_ref, s_ref, acc_ref, *, unroll):
    tq = qt_ref.shape[3]
    nkv, tk = vt_ref.shape[2], vt_ref.shape[4]
    row = lax.broadcasted_iota(jnp.int32, (LANES, QBLK), 0)
    lam_v = lam_ref[...]
    lam = (jnp.exp(jnp.sum(lam_v[0:1] * lam_v[1:2], axis=1, keepdims=True))
           - jnp.exp(jnp.sum(lam_v[2:3] * lam_v[3:4], axis=1, keepdims=True)) + LAM_INIT)
    for grp in range(tq // A_GROUP_Q):
        acc = acc_ref.at[grp]
        blocks = [grp * A_GROUP_Q + i * QBLK for i in range(A_GROUP_Q // QBLK)]
        qs = []
        for q0 in blocks:
            qt = qt_ref[0, 0, :, q0:q0 + QBLK]
            zero = jnp.zeros_like(qt)
            qs += [jnp.where(row < HALF, qt, zero), jnp.where(row >= HALF, qt, zero)]
        _flash_attention(
            lambda j: k_ref[0, 0, _key_rows(j, tk), :],
            lambda j, c: vt_ref[0, 0, j],
            qs, s_ref, acc, nkv, unroll)
        for i, q0 in enumerate(blocks):
            o1 = acc[2 * i, :A_V_DIM] / acc[2 * i, A_V_DIM:A_V_DIM + 1]
            o2 = acc[2 * i + 1, :A_V_DIM] / acc[2 * i + 1, A_V_DIM:A_V_DIM + 1]
            o = o1 - lam * o2
            y = _rms_rows(o.T, sg_ref[...]) * (1.0 - LAM_INIT)
            o_ref[0, q0:q0 + QBLK, :] = y.astype(BF16)


def _attn_a(lam4, sg, qat, ka, vat, *, tq, unroll):
    b, _, _, s = qat.shape
    nkv, tk = vat.shape[2], vat.shape[4]
    chains = 2 * A_GROUP_Q // QBLK
    return pl.pallas_call(
        functools.partial(_attn_a_kernel, unroll=unroll),
        grid=(b, A_HEADS, s // tq),
        in_specs=[
            _const_spec((4, A_HEAD_DIM)), _const_spec((1, A_V_DIM)),
            pl.BlockSpec((1, 1, LANES, tq), lambda bi, h, i: (bi, h, 0, i)),
            pl.BlockSpec((1, 1, s, LANES), lambda bi, h, i: (bi, h, 0, 0)),
            pl.BlockSpec((1, 1, nkv, VA_ROWS, tk), lambda bi, h, i: (bi, h, 0, 0, 0)),
        ],
        out_specs=pl.BlockSpec((1, tq, A_V_DIM), lambda bi, h, i: (bi, i, h)),
        out_shape=jax.ShapeDtypeStruct((b, s, A_HEADS * A_V_DIM), BF16),
        scratch_shapes=[pltpu.VMEM((chains, 2, tk, QBLK), F32),
                        pltpu.VMEM((tq // A_GROUP_Q, chains, VA_ROWS, QBLK), F32)],
        compiler_params=pltpu.CompilerParams(
            dimension_semantics=("parallel", "parallel", "arbitrary"), vmem_limit_bytes=VMEM_LIMIT),
        name="attn_a",
    )(lam4, sg, qat, ka, vat)


def _attn_b_kernel(qt_ref, k_ref, vt_ref, o_ref, s_ref, acc_ref, *, unroll):
    tq = qt_ref.shape[4]
    nkv, tk = vt_ref.shape[2], vt_ref.shape[4]
    dv = B_HEAD_DIM
    row = lax.broadcasted_iota(jnp.int32, (LANES, QBLK), 0)
    first = pl.program_id(2) * HALF
    keep = (row >= first) & (row < first + HALF)
    for grp in range(tq // QBLK):
        acc = acc_ref.at[grp]
        q0 = grp * QBLK
        qs = []
        for g in range(B_GROUP):
            qt = qt_ref[0, 0, g, :, q0:q0 + QBLK]
            qs.append(jnp.where(keep, qt, jnp.zeros_like(qt)))
        _flash_attention(lambda j: k_ref[0, 0, _key_rows(j, tk), :], lambda j, c: vt_ref[0, 0, j],
                         qs, s_ref, acc, nkv, unroll)
        for blk in range(B_GROUP // 2):
            parts = [acc[c, :dv] / acc[c, dv:dv + 1] for c in (2 * blk, 2 * blk + 1)]
            o_ref[0, q0:q0 + QBLK, blk * LANES:(blk + 1) * LANES] = (
                jnp.concatenate(parts, axis=0).T.astype(BF16))


def _attn_b(qbt, kb, vbt, *, tq, unroll):
    b, _, _, _, s = qbt.shape
    nkv, tk = vbt.shape[2], vbt.shape[4]
    width = B_GROUP * B_HEAD_DIM
    rows = B_HEAD_DIM + ONES_ROWS
    return pl.pallas_call(
        functools.partial(_attn_b_kernel, unroll=unroll),
        grid=(b, B_PAIRS, 2, s // tq),
        in_specs=[
            pl.BlockSpec((1, 1, B_GROUP, LANES, tq), lambda bi, j, hf, i: (bi, j, 0, 0, i)),
            pl.BlockSpec((1, 1, s, LANES), lambda bi, j, hf, i: (bi, j, 0, 0)),
            pl.BlockSpec((1, 1, nkv, rows, tk), lambda bi, j, hf, i: (bi, j, 0, hf, 0)),
        ],
        out_specs=pl.BlockSpec((1, tq, width), lambda bi, j, hf, i: (bi, i, 2 * j + hf)),
        out_shape=jax.ShapeDtypeStruct((b, s, B_HEADS * B_HEAD_DIM), BF16),
        scratch_shapes=[pltpu.VMEM((B_GROUP, 2, tk, QBLK), F32),
                        pltpu.VMEM((tq // QBLK, B_GROUP, rows, QBLK), F32)],
        compiler_params=pltpu.CompilerParams(
            dimension_semantics=("parallel", "parallel", "parallel", "arbitrary"), vmem_limit_bytes=VMEM_LIMIT),
        name="attn_b",
    )(qbt, kb, vbt)


def _merge_kernel(x_ref, oa_ref, ob_ref, sig_ref, wa_ref, wb_ref, wo_ref, o_ref):
    pa = jnp.dot(oa_ref[0], wa_ref[...], preferred_element_type=F32)
    pb = jnp.dot(ob_ref[0], wb_ref[...], preferred_element_type=F32)
    mix = sig_ref[0, :, :D_MODEL] * pa + sig_ref[0, :, D_MODEL:] * pb
    o_ref[0] = x_ref[0] + jnp.dot(mix.astype(BF16), wo_ref[...], preferred_element_type=F32)


def _merge(x, oa, ob, sig, wa, wb, wo, *, tm):
    b, s, _ = x.shape
    rows = lambda width: pl.BlockSpec((1, tm, width), lambda bi, i: (bi, i, 0))
    wspec = _const_spec((D_MODEL, D_MODEL))
    return pl.pallas_call(
        _merge_kernel,
        grid=(b, s // tm),
        in_specs=[rows(D_MODEL), rows(D_MODEL), rows(D_MODEL), rows(GATE_W), wspec, wspec, wspec],
        out_specs=rows(D_MODEL),
        out_shape=jax.ShapeDtypeStruct((b, s, D_MODEL), F32),
        compiler_params=pltpu.CompilerParams(
            dimension_semantics=("parallel", "parallel"), vmem_limit_bytes=VMEM_LIMIT),
        name="merge",
    )(x, oa, ob, sig, wa, wb, wo)


def _ffn_kernel(xp_ref, x_ref, xn_ref, g_ref, wu_ref, cw_ref, cb_ref, wd_ref, gf_ref, o_ref, h_ref, *, ck):
    tm = x_ref.shape[1]
    i = pl.program_id(1)
    g = g_ref[...]
    x = x_ref[0]
    hp = jnp.where(i > 0, _rms_rows(xp_ref[0], g), 0.0)
    hn = jnp.where(i < pl.num_programs(1) - 1, _rms_rows(xn_ref[0], g), 0.0)
    h_ref[0:HALO] = hp.astype(BF16)
    h_ref[HALO:HALO + tm] = _rms_rows(x, g).astype(BF16)
    h_ref[HALO + tm:] = hn.astype(BF16)
    h = h_ref[...]

    def conv(u, c0):
        w = cw_ref[:, c0:c0 + ck]
        rows = u.shape[0]
        prev = pltpu.roll(u, 1, 0)[HALO:HALO + tm]
        nxt = pltpu.roll(u, rows - 1, 0)[HALO:HALO + tm]
        return prev * w[0:1] + u[HALO:HALO + tm] * w[1:2] + nxt * w[2:3] + cb_ref[:, c0:c0 + ck]

    def up(c):
        return (jnp.dot(h, wu_ref[:, c * ck:(c + 1) * ck], preferred_element_type=F32),
                jnp.dot(h, wu_ref[:, D_FF + c * ck:D_FF + (c + 1) * ck], preferred_element_type=F32))

    nc = D_FF // ck
    acc = jnp.zeros((tm, D_MODEL), F32)
    u_next = up(0)
    act_prev = None
    for c in range(nc):
        uv, ug = u_next
        if c + 1 < nc:
            u_next = up(c + 1)
        if act_prev is not None:
            acc = acc + jnp.dot(act_prev, wd_ref[(c - 1) * ck:c * ck, :], preferred_element_type=F32)
        act_prev = (jax.nn.silu(conv(ug, D_FF + c * ck)) * conv(uv, c * ck)).astype(BF16)
    acc = acc + jnp.dot(act_prev, wd_ref[(nc - 1) * ck:nc * ck, :], preferred_element_type=F32)
    o_ref[0] = _rms_rows(x + acc, gf_ref[...])


def _ffn(x, g, wu, cw, cb, wd, gf, *, tm, ck):
    b, s, _ = x.shape
    nh = tm // HALO
    last = s // HALO - 1
    return pl.pallas_call(
        functools.partial(_ffn_kernel, ck=ck),
        grid=(b, s // tm),
        in_specs=[
            pl.BlockSpec((1, HALO, D_MODEL), lambda bi, i: (bi, jnp.maximum(i * nh - 1, 0), 0)),
            pl.BlockSpec((1, tm, D_MODEL), lambda bi, i: (bi, i, 0)),
            pl.BlockSpec((1, HALO, D_MODEL), lambda bi, i: (bi, jnp.minimum((i + 1) * nh, last), 0)),
            _const_spec((1, D_MODEL)),
            _const_spec((D_MODEL, 2 * D_FF)),
            _const_spec((3, 2 * D_FF)),
            _const_spec((1, 2 * D_FF)),
            _const_spec((D_FF, D_MODEL)),
            _const_spec((1, D_MODEL)),
        ],
        out_specs=pl.BlockSpec((1, tm, D_MODEL), lambda bi, i: (bi, i, 0)),
        out_shape=jax.ShapeDtypeStruct((b, s, D_MODEL), F32),
        scratch_shapes=[pltpu.VMEM((tm + 2 * HALO, D_MODEL), BF16)],
        compiler_params=pltpu.CompilerParams(
            dimension_semantics=("parallel", "parallel"), vmem_limit_bytes=VMEM_LIMIT),
        name="ffn",
    )(x, x, x, g, wu, cw, cb, wd, gf)


def _rope_angles(pos, dim, theta):
    inv = 1.0 / (theta ** (jnp.arange(0, dim, 2, dtype=F32) / dim))
    ang = pos.astype(F32)[:, None] * inv[None, :]
    return jnp.cos(ang), jnp.sin(ang)


def _rope_tables(s):
    pos = jnp.arange(s)
    ca, sa = _rope_angles(pos, A_ROT_DIM, A_ROPE_THETA)
    ones = jnp.ones((s, A_HEAD_DIM - A_ROT_DIM), F32)
    cos_a = jnp.concatenate([ca, ca, ones], axis=1)
    sin_a = jnp.concatenate([-sa, sa, 0.0 * ones], axis=1)
    half = B_HEAD_DIM // 2
    cr, sr = _rope_angles(pos // GRID_W, half, B_ROPE_THETA)
    cc, sc = _rope_angles(pos % GRID_W, half, B_ROPE_THETA)
    cos_b = jnp.concatenate([cr, cr, cc, cc], axis=1)
    sin_b = jnp.concatenate([-sr, sr, -sc, sc], axis=1)
    two = lambda t: jnp.concatenate([t, t], axis=1)
    return two(cos_a), two(sin_a), two(cos_b), two(sin_b)


def _w_in_columns():
    cols = list(range(C_QB))
    for j in range(B_PAIRS):
        for g in range(B_GROUP):
            for half in range(2):
                head = (2 * j + half) * B_GROUP + g
                cols.extend(range(C_QB + head * B_HEAD_DIM, C_QB + (head + 1) * B_HEAD_DIM))
    cols.extend(range(C_KB, D_IN))
    return jnp.asarray(cols, dtype=jnp.int32)


_T_COLS = list(range(C_QA, C_KA)) + list(range(C_VA, C_KB)) + list(range(C_VB, C_GATE))
_N_COLS = list(range(C_KA, C_VA)) + list(range(C_KB, C_VB)) + list(range(C_GATE, D_IN))


def _tile(n, pref):
    return pref if n % pref == 0 else n


IN_TM = 512


def _trunk(x, p, tabs, tabs_t):
    s = x.shape[1]
    tk = 512
    nkv = s // tk
    qat, ka, vat, qbt, kb, vbt, sig = _inproj(x, p["norm_mix_g"], p["w_in_t"], p["w_in_n"], tabs, tabs_t,
                                              p["q_norm_gt"], p["k_norm_g"], tm=IN_TM, tk=tk)
    unroll = 8 if nkv % 8 == 0 else 2
    oa = _attn_a(p["lam4"], p["subln_g"], qat, ka, vat, tq=4 * A_GROUP_Q, unroll=unroll)
    ob = _attn_b(qbt, kb, vbt, tq=4 * QBLK, unroll=unroll)
    x1 = _merge(x, oa, ob, sig, p["w_proj_a"], p["w_proj_b"], p["w_out"], tm=_tile(s, 512))
    return _ffn(x1, p["norm_ffn_g"], p["w_up"], p["conv_w"], p["conv_b"], p["w_down"], p["norm_final_g"],
                tm=_tile(s, 1024), ck=256)


def kernel(x_prompt, x_sample, norm_mix_g, w_in, lam_q1, lam_k1, lam_q2, lam_k2, subln_g, q_norm_g, k_norm_g,
           w_proj_a, w_proj_b, w_out, norm_ffn_g, w_up, conv_w, conv_b, w_down, norm_final_g):
    two = lambda t: jnp.concatenate([t, t]).reshape(1, LANES).astype(F32)
    p = {
        "norm_mix_g": norm_mix_g[0].reshape(1, D_MODEL),
        "w_in_t": jnp.take(w_in[0], _w_in_columns()[jnp.asarray(_T_COLS)], axis=1).T.astype(BF16),
        "w_in_n": jnp.take(w_in[0], _w_in_columns()[jnp.asarray(_N_COLS)], axis=1).astype(BF16),
        "lam4": jnp.stack([lam_q1[0], lam_k1[0], lam_q2[0], lam_k2[0]]).astype(F32),
        "subln_g": subln_g[0].reshape(1, A_V_DIM),
        "q_norm_gt": jnp.broadcast_to(two(q_norm_g[0]).reshape(LANES, 1), (LANES, IN_TM)),
        "k_norm_g": two(k_norm_g[0]),
        "w_proj_a": w_proj_a[0].astype(BF16),
        "w_proj_b": w_proj_b[0].astype(BF16),
        "w_out": w_out[0].astype(BF16),
        "norm_ffn_g": norm_ffn_g[0].reshape(1, D_MODEL),
        "w_up": w_up[0].astype(BF16),
        "conv_w": conv_w[0],
        "conv_b": conv_b[0].reshape(1, 2 * D_FF),
        "w_down": w_down[0].astype(BF16),
        "norm_final_g": norm_final_g.reshape(1, D_MODEL),
    }
    tables = {}
    outs = []
    for x in (x_prompt, x_sample):
        s = x.shape[1]
        if s not in tables:
            tabs = _rope_tables(s)
            tables[s] = (tabs, tuple(t.T for t in tabs))
        outs.append(_trunk(x, p, *tables[s]))
    return tuple(outs)
```

```python
import functools
import math

import jax
import jax.numpy as jnp
from jax import lax
from jax.experimental import pallas as pl
from jax.experimental.pallas import tpu as pltpu

D_MODEL = 1024
GRID_W = 64
EPS = 1e-6
A_HEADS = 8
A_HEAD_DIM = 64
A_V_DIM = 128
A_ROT_DIM = 16
A_ROPE_THETA = 500000.0
B_HEADS = 16
B_KV_HEADS = 4
B_GROUP = 4
B_HEAD_DIM = 64
B_ROPE_THETA = 10000.0
D_FF = 2816
LAM_INIT = 0.8 - 0.6 * math.exp(-0.3 * 0)

A_Q = A_HEADS * 2 * A_HEAD_DIM
A_K = A_Q
A_V = A_HEADS * A_V_DIM
B_Q = B_HEADS * B_HEAD_DIM
B_K = B_KV_HEADS * B_HEAD_DIM
B_V = B_K
GATE_W = 2 * D_MODEL
D_IN = A_Q + A_K + A_V + B_Q + B_K + B_V + GATE_W
C_QA = 0
C_KA = C_QA + A_Q
C_VA = C_KA + A_K
C_QB = C_VA + A_V
C_KB = C_QB + B_Q
C_VB = C_KB + B_K
C_GATE = C_VB + B_V

LANES = 128
HALF = 64
B_PAIRS = B_KV_HEADS // 2
HALO = 16
QBLK = 256
A_GROUP_Q = 2 * QBLK
ONES_ROWS = 16
VA_ROWS = A_V_DIM + ONES_ROWS
VB_ROWS = 2 * (B_HEAD_DIM + ONES_ROWS)
VMEM_LIMIT = 56 * 1024 * 1024

F32 = jnp.float32
BF16 = jnp.bfloat16


def _rms_rows(x, g):
    return x * lax.rsqrt(jnp.mean(x * x, axis=-1, keepdims=True) + EPS) * g


def _const_spec(shape):
    zeros = (0,) * len(shape)
    return pl.BlockSpec(shape, lambda *_: zeros, pipeline_mode=pl.Buffered(1))


def _inproj_kernel(x_ref, g_ref, wt_ref, wn_ref, ca_ref, sa_ref, cb_ref, sb_ref,
                     cat_ref, sat_ref, cbt_ref, sbt_ref, qgt_ref, kg_ref,
                     qat_ref, ka_ref, vat_ref, qbt_ref, kb_ref, vbt_ref, sig_ref, ht_ref):
    tm = x_ref.shape[1]
    hn = _rms_rows(x_ref[0], g_ref[...])
    h = hn.astype(BF16)
    for blk in range(D_MODEL // LANES):
        ht_ref[blk * LANES:(blk + 1) * LANES, :] = hn[:, blk * LANES:(blk + 1) * LANES].T.astype(BF16)
    ht = ht_ref[...]
    lane = lax.broadcasted_iota(jnp.int32, (tm, LANES), 1)
    lo_half = lane < HALF
    first8 = (lane & 8) == 0
    first16 = (lane & 16) == 0
    ca, sa, cb, sb = ca_ref[...], sa_ref[...], cb_ref[...], sb_ref[...]
    cat, sat, cbt, sbt = cat_ref[...], sat_ref[...], cbt_ref[...], sbt_ref[...]
    qgt, kg = qgt_ref[...], kg_ref[...]
    scale = A_HEAD_DIM ** -0.5 * math.log2(math.e)
    ones_rows = jnp.ones((ONES_ROWS, tm), BF16)

    def swap_rows(t, width):
        parts = [t[r:r + width] for r in range(0, t.shape[0], width)]
        return jnp.concatenate([parts[i ^ 1] for i in range(len(parts))], axis=0)

    def rope_a_t(t):
        return t * cat + swap_rows(t, 8) * sat

    def rope_b_t(t):
        return t * cbt + swap_rows(t, 16) * sbt

    def headnorm_t(t, g):
        halves = []
        for r0 in (0, HALF):
            th = t[r0:r0 + HALF]
            ms = jnp.mean(th * th, axis=0, keepdims=True)
            halves.append(th * lax.rsqrt(ms + EPS) * g[r0:r0 + HALF])
        return jnp.concatenate(halves, axis=0)

    def tproj(r0, n):
        return jnp.dot(wt_ref[r0:r0 + n, :], ht, preferred_element_type=F32)

    acc = tproj(0, A_Q)
    for hd in range(A_HEADS):
        qat_ref[0, hd] = (rope_a_t(acc[hd * LANES:(hd + 1) * LANES]) * scale).astype(BF16)
    acc = tproj(A_Q, A_V)
    for hd in range(A_HEADS):
        vat_ref[0, hd, 0, :A_V_DIM] = acc[hd * LANES:(hd + 1) * LANES].astype(BF16)
        vat_ref[0, hd, 0, A_V_DIM:] = ones_rows
    acc = tproj(A_Q + A_V, B_Q)
    for blk in range(B_PAIRS * B_GROUP):
        t = rope_b_t(headnorm_t(acc[blk * LANES:(blk + 1) * LANES], qgt)) * scale
        qbt_ref[0, blk // B_GROUP, blk % B_GROUP] = t.astype(BF16)
    acc = tproj(A_Q + A_V + B_Q, B_V)
    for j in range(B_PAIRS):
        for half in range(2):
            r0 = half * (B_HEAD_DIM + ONES_ROWS)
            src = j * LANES + half * B_HEAD_DIM
            vbt_ref[0, j, 0, r0:r0 + B_HEAD_DIM] = acc[src:src + B_HEAD_DIM].astype(BF16)
            vbt_ref[0, j, 0, r0 + B_HEAD_DIM:r0 + B_HEAD_DIM + ONES_ROWS] = ones_rows

    def proj(c0, n):
        return jnp.dot(h, wn_ref[:, c0:c0 + n], preferred_element_type=F32)

    def rope_a(t):
        partner = jnp.where(first8, pltpu.roll(t, LANES - 8, 1), pltpu.roll(t, 8, 1))
        return t * ca + partner * sa

    def rope_b(t):
        partner = jnp.where(first16, pltpu.roll(t, LANES - 16, 1), pltpu.roll(t, 16, 1))
        return t * cb + partner * sb

    def headnorm(t, g):
        sq = t * t
        s_lo = jnp.sum(jnp.where(lo_half, sq, 0.0), axis=1, keepdims=True)
        s_hi = jnp.sum(jnp.where(lo_half, 0.0, sq), axis=1, keepdims=True)
        ms = jnp.where(lo_half, s_lo, s_hi) * (1.0 / HALF)
        return t * lax.rsqrt(ms + EPS) * g

    for p in range(A_HEADS // 2):
        acc = proj(2 * LANES * p, 2 * LANES)
        for i in range(2):
            ka_ref[0, 2 * p + i] = rope_a(acc[:, i * LANES:(i + 1) * LANES]).astype(BF16)
    acc = proj(A_K, B_K)
    for j in range(B_PAIRS):
        kb_ref[0, j] = rope_b(headnorm(acc[:, j * LANES:(j + 1) * LANES], kg)).astype(BF16)
    for p in range(GATE_W // (2 * LANES)):
        acc = proj(A_K + B_K + 2 * LANES * p, 2 * LANES)
        sig_ref[0, :, 2 * LANES * p:2 * LANES * (p + 1)] = jax.nn.sigmoid(acc).astype(sig_ref.dtype)


def _inproj(x, g, wt, wn, tabs, tabs_t, qgt, kg, *, tm, tk):
    b, s, _ = x.shape
    nt = s // tk
    sub = tk // tm
    row = lambda bi, i: (i, 0)
    col = lambda bi, i: (0, i)
    out_shapes = (
        jax.ShapeDtypeStruct((b, A_HEADS, LANES, s), BF16),
        jax.ShapeDtypeStruct((b, A_HEADS, s, LANES), BF16),
        jax.ShapeDtypeStruct((b, A_HEADS, nt, VA_ROWS, tk), BF16),
        jax.ShapeDtypeStruct((b, B_PAIRS, B_GROUP, LANES, s), BF16),
        jax.ShapeDtypeStruct((b, B_PAIRS, s, LANES), BF16),
        jax.ShapeDtypeStruct((b, B_PAIRS, nt, VB_ROWS, tk), BF16),
        jax.ShapeDtypeStruct((b, s, GATE_W), BF16),
    )
    out_specs = (
        pl.BlockSpec((1, A_HEADS, LANES, tm), lambda bi, i: (bi, 0, 0, i)),
        pl.BlockSpec((1, A_HEADS, tm, LANES), lambda bi, i: (bi, 0, i, 0)),
        pl.BlockSpec((1, A_HEADS, 1, VA_ROWS, tm), lambda bi, i: (bi, 0, i // sub, 0, i % sub)),
        pl.BlockSpec((1, B_PAIRS, B_GROUP, LANES, tm), lambda bi, i: (bi, 0, 0, 0, i)),
        pl.BlockSpec((1, B_PAIRS, tm, LANES), lambda bi, i: (bi, 0, i, 0)),
        pl.BlockSpec((1, B_PAIRS, 1, VB_ROWS, tm), lambda bi, i: (bi, 0, i // sub, 0, i % sub)),
        pl.BlockSpec((1, tm, GATE_W), lambda bi, i: (bi, i, 0)),
    )
    in_specs = [
        pl.BlockSpec((1, tm, D_MODEL), lambda bi, i: (bi, i, 0)),
        _const_spec((1, D_MODEL)),
        _const_spec(wt.shape),
        _const_spec(wn.shape),
        pl.BlockSpec((tm, LANES), row), pl.BlockSpec((tm, LANES), row),
        pl.BlockSpec((tm, LANES), row), pl.BlockSpec((tm, LANES), row),
        pl.BlockSpec((LANES, tm), col), pl.BlockSpec((LANES, tm), col),
        pl.BlockSpec((LANES, tm), col), pl.BlockSpec((LANES, tm), col),
        _const_spec((LANES, tm)), _const_spec((1, LANES)),
    ]
    return pl.pallas_call(
        _inproj_kernel, grid=(b, s // tm), in_specs=in_specs, out_specs=out_specs, out_shape=out_shapes,
        scratch_shapes=[pltpu.VMEM((D_MODEL, tm), BF16)],
        compiler_params=pltpu.CompilerParams(
            dimension_semantics=("parallel", "parallel"), vmem_limit_bytes=VMEM_LIMIT),
        name="inproj",
    )(x, g, wt, wn, *tabs, *tabs_t, qgt, kg)


def _flash_attention(k_tile, vt_tile, qs, s_ref, acc_ref, nkv, unroll):
    n = len(qs)
    assert unroll % 2 == 0 and nkv % unroll == 0

    def score(k, c, slot):
        s = jnp.dot(k, qs[c], preferred_element_type=F32)
        s_ref[c, slot] = s
        return jnp.max(s, axis=0, keepdims=True)

    def scores(j, slot):
        k = k_tile(j)
        return tuple(score(k, c, slot) for c in range(n))

    def update(j, c, slot, mx, m):
        m_new = jnp.maximum(m, mx)
        alpha = jnp.exp2(m - m_new)
        p = jnp.exp2(s_ref[c, slot] - m_new).astype(BF16)
        acc_ref[c] = alpha * acc_ref[c] + jnp.dot(vt_tile(j, c), p, preferred_element_type=F32)
        return m_new

    acc_ref[...] = jnp.zeros(acc_ref.shape, F32)
    tq = qs[0].shape[1]
    m0 = (jnp.full((1, tq), -jnp.inf, F32),) * n

    def tiles(j0, m, mx, last):
        m = list(m)
        for t in range(unroll):
            skip = last and t == unroll - 1
            k = None if skip else k_tile(j0 + t + 1)
            mx_next = []
            for c in range(n):
                if not skip:
                    mx_next.append(score(k, c, (t + 1) % 2))
                m[c] = update(j0 + t, c, t % 2, mx[c], m[c])
            mx = tuple(mx_next)
        return tuple(m), mx

    carry = (m0, scores(0, 0))
    carry = lax.fori_loop(0, nkv // unroll - 1, lambda i, c: tiles(i * unroll, *c, False), carry)
    tiles(nkv - unroll, *carry, True)


def _key_rows(j, tk):
    start = j * tk
    return pl.ds(start if isinstance(start, int) else pl.multiple_of(start, tk), tk)


def _attn_a_kernel(lam_ref, sg_ref, qt_ref, k_ref, vt_ref, o_ref, s_ref, acc_ref, *, unroll):
    tq = qt_ref.shape[3]
    nkv, tk = vt_ref.shape[2], vt_ref.shape[4]
    row = lax.broadcasted_iota(jnp.int32, (LANES, QBLK), 0)
    lam_v = lam_ref[...]
    lam = (jnp.exp(jnp.sum(lam_v[0:1] * lam_v[1:2], axis=1, keepdims=True))
           - jnp.exp(jnp.sum(lam_v[2:3] * lam_v[3:4], axis=1, keepdims=True)) + LAM_INIT)
    for grp in range(tq // A_GROUP_Q):
        acc = acc_ref.at[grp]
        blocks = [grp * A_GROUP_Q + i * QBLK for i in range(A_GROUP_Q // QBLK)]
        qs = []
        for q0 in blocks:
            qt = qt_ref[0, 0, :, q0:q0 + QBLK]
            zero = jnp.zeros_like(qt)
            qs += [jnp.where(row < HALF, qt, zero), jnp.where(row >= HALF, qt, zero)]
        _flash_attention(
            lambda j: k_ref[0, 0, _key_rows(j, tk), :],
            lambda j, c: vt_ref[0, 0, j],
            qs, s_ref, acc, nkv, unroll)
        for i, q0 in enumerate(blocks):
            o1 = acc[2 * i, :A_V_DIM] / acc[2 * i, A_V_DIM:A_V_DIM + 1]
            o2 = acc[2 * i + 1, :A_V_DIM] / acc[2 * i + 1, A_V_DIM:A_V_DIM + 1]
            o = o1 - lam * o2
            y = _rms_rows(o.T, sg_ref[...]) * (1.0 - LAM_INIT)
            o_ref[0, q0:q0 + QBLK, :] = y.astype(BF16)


def _attn_a(lam4, sg, qat, ka, vat, *, tq, unroll):
    b, _, _, s = qat.shape
    nkv, tk = vat.shape[2], vat.shape[4]
    chains = 2 * A_GROUP_Q // QBLK
    return pl.pallas_call(
        functools.partial(_attn_a_kernel, unroll=unroll),
        grid=(b, A_HEADS, s // tq),
        in_specs=[
            _const_spec((4, A_HEAD_DIM)), _const_spec((1, A_V_DIM)),
            pl.BlockSpec((1, 1, LANES, tq), lambda bi, h, i: (bi, h, 0, i)),
            pl.BlockSpec((1, 1, s, LANES), lambda bi, h, i: (bi, h, 0, 0)),
            pl.BlockSpec((1, 1, nkv, VA_ROWS, tk), lambda bi, h, i: (bi, h, 0, 0, 0)),
        ],
        out_specs=pl.BlockSpec((1, tq, A_V_DIM), lambda bi, h, i: (bi, i, h)),
        out_shape=jax.ShapeDtypeStruct((b, s, A_HEADS * A_V_DIM), BF16),
        scratch_shapes=[pltpu.VMEM((chains, 2, tk, QBLK), F32),
                        pltpu.VMEM((tq // A_GROUP_Q, chains, VA_ROWS, QBLK), F32)],
        compiler_params=pltpu.CompilerParams(
            dimension_semantics=("parallel", "parallel", "arbitrary"), vmem_limit_bytes=VMEM_LIMIT),
        name="attn_a",
    )(lam4, sg, qat, ka, vat)


def _attn_b_kernel(qt_ref, k_ref, vt_ref, o_ref, s_ref, acc_ref, *, unroll):
    tq = qt_ref.shape[4]
    nkv, tk = vt_ref.shape[2], vt_ref.shape[4]
    dv = B_HEAD_DIM
    row = lax.broadcasted_iota(jnp.int32, (LANES, QBLK), 0)
    first = pl.program_id(2) * HALF
    keep = (row >= first) & (row < first + HALF)
    for grp in range(tq // QBLK):
        acc = acc_ref.at[grp]
        q0 = grp * QBLK
        qs = []
        for g in range(B_GROUP):
            qt = qt_ref[0, 0, g, :, q0:q0 + QBLK]
            qs.append(jnp.where(keep, qt, jnp.zeros_like(qt)))
        _flash_attention(lambda j: k_ref[0, 0, _key_rows(j, tk), :], lambda j, c: vt_ref[0, 0, j],
                         qs, s_ref, acc, nkv, unroll)
        for blk in range(B_GROUP // 2):
            parts = [acc[c, :dv] / acc[c, dv:dv + 1] for c in (2 * blk, 2 * blk + 1)]
            o_ref[0, q0:q0 + QBLK, blk * LANES:(blk + 1) * LANES] = (
                jnp.concatenate(parts, axis=0).T.astype(BF16))


def _attn_b(qbt, kb, vbt, *, tq, unroll):
    b, _, _, _, s = qbt.shape
    nkv, tk = vbt.shape[2], vbt.shape[4]
    width = B_GROUP * B_HEAD_DIM
    rows = B_HEAD_DIM + ONES_ROWS
    return pl.pallas_call(
        functools.partial(_attn_b_kernel, unroll=unroll),
        grid=(b, B_PAIRS, 2, s // tq),
        in_specs=[
            pl.BlockSpec((1, 1, B_GROUP, LANES, tq), lambda bi, j, hf, i: (bi, j, 0, 0, i)),
            pl.BlockSpec((1, 1, s, LANES), lambda bi, j, hf, i: (bi, j, 0, 0)),
            pl.BlockSpec((1, 1, nkv, rows, tk), lambda bi, j, hf, i: (bi, j, 0, hf, 0)),
        ],
        out_specs=pl.BlockSpec((1, tq, width), lambda bi, j, hf, i: (bi, i, 2 * j + hf)),
        out_shape=jax.ShapeDtypeStruct((b, s, B_HEADS * B_HEAD_DIM), BF16),
        scratch_shapes=[pltpu.VMEM((B_GROUP, 2, tk, QBLK), F32),
                        pltpu.VMEM((tq // QBLK, B_GROUP, rows, QBLK), F32)],
        compiler_params=pltpu.CompilerParams(
            dimension_semantics=("parallel", "parallel", "parallel", "arbitrary"), vmem_limit_bytes=VMEM_LIMIT),
        name="attn_b",
    )(qbt, kb, vbt)


def _merge_kernel(x_ref, oa_ref, ob_ref, sig_ref, wa_ref, wb_ref, wo_ref, o_ref):
    pa = jnp.dot(oa_ref[0], wa_ref[...], preferred_element_type=F32)
    pb = jnp.dot(ob_ref[0], wb_ref[...], preferred_element_type=F32)
    mix = sig_ref[0, :, :D_MODEL] * pa + sig_ref[0, :, D_MODEL:] * pb
    o_ref[0] = x_ref[0] + jnp.dot(mix.astype(BF16), wo_ref[...], preferred_element_type=F32)


def _merge(x, oa, ob, sig, wa, wb, wo, *, tm):
    b, s, _ = x.shape
    rows = lambda width: pl.BlockSpec((1, tm, width), lambda bi, i: (bi, i, 0))
    wspec = _const_spec((D_MODEL, D_MODEL))
    return pl.pallas_call(
        _merge_kernel,
        grid=(b, s // tm),
        in_specs=[rows(D_MODEL), rows(D_MODEL), rows(D_MODEL), rows(GATE_W), wspec, wspec, wspec],
        out_specs=rows(D_MODEL),
        out_shape=jax.ShapeDtypeStruct((b, s, D_MODEL), F32),
        compiler_params=pltpu.CompilerParams(
            dimension_semantics=("parallel", "parallel"), vmem_limit_bytes=VMEM_LIMIT),
        name="merge",
    )(x, oa, ob, sig, wa, wb, wo)


def _ffn_kernel(xp_ref, x_ref, xn_ref, g_ref, wu_ref, cw_ref, cb_ref, wd_ref, gf_ref, o_ref, h_ref, *, ck):
    tm = x_ref.shape[1]
    i = pl.program_id(1)
    g = g_ref[...]
    x = x_ref[0]
    hp = jnp.where(i > 0, _rms_rows(xp_ref[0], g), 0.0)
    hn = jnp.where(i < pl.num_programs(1) - 1, _rms_rows(xn_ref[0], g), 0.0)
    h_ref[0:HALO] = hp.astype(BF16)
    h_ref[HALO:HALO + tm] = _rms_rows(x, g).astype(BF16)
    h_ref[HALO + tm:] = hn.astype(BF16)
    h = h_ref[...]

    def conv(u, c0):
        rows, width = u.shape
        w = cw_ref[:, c0:c0 + width]
        prev = pltpu.roll(u, 1, 0)[HALO:HALO + tm]
        nxt = pltpu.roll(u, rows - 1, 0)[HALO:HALO + tm]
        return prev * w[0:1] + u[HALO:HALO + tm] * w[1:2] + nxt * w[2:3] + cb_ref[:, c0:c0 + width]

    def up(c):
        return (jnp.dot(h, wu_ref[:, c * ck:(c + 1) * ck], preferred_element_type=F32),
                jnp.dot(h, wu_ref[:, D_FF + c * ck:D_FF + (c + 1) * ck], preferred_element_type=F32))

    nc = D_FF // ck
    acc = jnp.zeros((tm, D_MODEL), F32)
    u_next = up(0)
    act = None
    for c in range(nc):
        uv, ug = u_next
        if c + 1 < nc:
            u_next = up(c + 1)
        if act is not None:
            acc = acc + jnp.dot(act, wd_ref[(c - 1) * ck:c * ck, :], preferred_element_type=F32)
        halves = []
        for l0 in range(0, ck, LANES):
            gate = conv(ug[:, l0:l0 + LANES], D_FF + c * ck + l0)
            halves.append((jax.nn.silu(gate) * conv(uv[:, l0:l0 + LANES], c * ck + l0)).astype(BF16))
        act = jnp.concatenate(halves, axis=1)
    acc = acc + jnp.dot(act, wd_ref[(nc - 1) * ck:nc * ck, :], preferred_element_type=F32)
    o_ref[0] = _rms_rows(x + acc, gf_ref[...])


def _ffn(x, g, wu, cw, cb, wd, gf, *, tm, ck):
    b, s, _ = x.shape
    nh = tm // HALO
    last = s // HALO - 1
    return pl.pallas_call(
        functools.partial(_ffn_kernel, ck=ck),
        grid=(b, s // tm),
        in_specs=[
            pl.BlockSpec((1, HALO, D_MODEL), lambda bi, i: (bi, jnp.maximum(i * nh - 1, 0), 0)),
            pl.BlockSpec((1, tm, D_MODEL), lambda bi, i: (bi, i, 0)),
            pl.BlockSpec((1, HALO, D_MODEL), lambda bi, i: (bi, jnp.minimum((i + 1) * nh, last), 0)),
            _const_spec((1, D_MODEL)),
            _const_spec((D_MODEL, 2 * D_FF)),
            _const_spec((3, 2 * D_FF)),
            _const_spec((1, 2 * D_FF)),
            _const_spec((D_FF, D_MODEL)),
            _const_spec((1, D_MODEL)),
        ],
        out_specs=pl.BlockSpec((1, tm, D_MODEL), lambda bi, i: (bi, i, 0)),
        out_shape=jax.ShapeDtypeStruct((b, s, D_MODEL), F32),
        scratch_shapes=[pltpu.VMEM((tm + 2 * HALO, D_MODEL), BF16)],
        compiler_params=pltpu.CompilerParams(
            dimension_semantics=("parallel", "parallel"), vmem_limit_bytes=VMEM_LIMIT),
        name="ffn",
    )(x, x, x, g, wu, cw, cb, wd, gf)


def _rope_angles(pos, dim, theta):
    inv = 1.0 / (theta ** (jnp.arange(0, dim, 2, dtype=F32) / dim))
    ang = pos.astype(F32)[:, None] * inv[None, :]
    return jnp.cos(ang), jnp.sin(ang)


def _rope_tables(s):
    pos = jnp.arange(s)
    ca, sa = _rope_angles(pos, A_ROT_DIM, A_ROPE_THETA)
    ones = jnp.ones((s, A_HEAD_DIM - A_ROT_DIM), F32)
    cos_a = jnp.concatenate([ca, ca, ones], axis=1)
    sin_a = jnp.concatenate([-sa, sa, 0.0 * ones], axis=1)
    half = B_HEAD_DIM // 2
    cr, sr = _rope_angles(pos // GRID_W, half, B_ROPE_THETA)
    cc, sc = _rope_angles(pos % GRID_W, half, B_ROPE_THETA)
    cos_b = jnp.concatenate([cr, cr, cc, cc], axis=1)
    sin_b = jnp.concatenate([-sr, sr, -sc, sc], axis=1)
    two = lambda t: jnp.concatenate([t, t], axis=1)
    return two(cos_a), two(sin_a), two(cos_b), two(sin_b)


def _w_in_columns():
    cols = list(range(C_QB))
    for j in range(B_PAIRS):
        for g in range(B_GROUP):
            for half in range(2):
                head = (2 * j + half) * B_GROUP + g
                cols.extend(range(C_QB + head * B_HEAD_DIM, C_QB + (head + 1) * B_HEAD_DIM))
    cols.extend(range(C_KB, D_IN))
    return jnp.asarray(cols, dtype=jnp.int32)


_T_COLS = list(range(C_QA, C_KA)) + list(range(C_VA, C_KB)) + list(range(C_VB, C_GATE))
_N_COLS = list(range(C_KA, C_VA)) + list(range(C_KB, C_VB)) + list(range(C_GATE, D_IN))


def _tile(n, pref):
    return pref if n % pref == 0 else n


IN_TM = 512


def _trunk(x, p, tabs, tabs_t):
    s = x.shape[1]
    tk = 512
    nkv = s // tk
    qat, ka, vat, qbt, kb, vbt, sig = _inproj(x, p["norm_mix_g"], p["w_in_t"], p["w_in_n"], tabs, tabs_t,
                                              p["q_norm_gt"], p["k_norm_g"], tm=IN_TM, tk=tk)
    unroll = 8 if nkv % 8 == 0 else 2
    oa = _attn_a(p["lam4"], p["subln_g"], qat, ka, vat, tq=4 * A_GROUP_Q, unroll=unroll)
    ob = _attn_b(qbt, kb, vbt, tq=4 * QBLK, unroll=unroll)
    x1 = _merge(x, oa, ob, sig, p["w_proj_a"], p["w_proj_b"], p["w_out"], tm=_tile(s, 512))
    return _ffn(x1, p["norm_ffn_g"], p["w_up"], p["conv_w"], p["conv_b"], p["w_down"], p["norm_final_g"],
                tm=_tile(s, 1024), ck=256)


def kernel(x_prompt, x_sample, norm_mix_g, w_in, lam_q1, lam_k1, lam_q2, lam_k2, subln_g, q_norm_g, k_norm_g,
           w_proj_a, w_proj_b, w_out, norm_ffn_g, w_up, conv_w, conv_b, w_down, norm_final_g):
    two = lambda t: jnp.concatenate([t, t]).reshape(1, LANES).astype(F32)
    p = {
        "norm_mix_g": norm_mix_g[0].reshape(1, D_MODEL),
        "w_in_t": jnp.take(w_in[0], _w_in_columns()[jnp.asarray(_T_COLS)], axis=1).T.astype(BF16),
        "w_in_n": jnp.take(w_in[0], _w_in_columns()[jnp.asarray(_N_COLS)], axis=1).astype(BF16),
        "lam4": jnp.stack([lam_q1[0], lam_k1[0], lam_q2[0], lam_k2[0]]).astype(F32),
        "subln_g": subln_g[0].reshape(1, A_V_DIM),
        "q_norm_gt": jnp.broadcast_to(two(q_norm_g[0]).reshape(LANES, 1), (LANES, IN_TM)),
        "k_norm_g": two(k_norm_g[0]),
        "w_proj_a": w_proj_a[0].astype(BF16),
        "w_proj_b": w_proj_b[0].astype(BF16),
        "w_out": w_out[0].astype(BF16),
        "norm_ffn_g": norm_ffn_g[0].reshape(1, D_MODEL),
        "w_up": w_up[0].astype(BF16),
        "conv_w": conv_w[0],
        "conv_b": conv_b[0].reshape(1, 2 * D_FF),
        "w_down": w_down[0].astype(BF16),
        "norm_final_g": norm_final_g.reshape(1, D_MODEL),
    }
    tables = {}
    outs = []
    for x in (x_prompt, x_sample):
        s = x.shape[1]
        if s not in tables:
            tabs = _rope_tables(s)
            tables[s] = (tabs, tuple(t.T for t in tabs))
        outs.append(_trunk(x, p, *tables[s]))
    return tuple(outs)
```
